```python
import jax, jax.numpy as jnp
from jax import lax
import numpy as np

D_MODEL = 2048
BATCH = 16
SEQ = 2048
DEPTH = 4

CHUNK = 64
N_MIXERS = 3

DN_ALPHA = (2 * DEPTH) ** 0.25
DN_BETA = (8 * DEPTH) ** -0.25
LN_EPS = 1e-5

RG_WIDTH = 5 * D_MODEL // 4
RG_BLOCKS = 16
RG_BLOCK = RG_WIDTH // RG_BLOCKS
RG_C = 8.0
CONV_W = 4

SG_BLOCK = 128
SG_HALF = D_MODEL
SG_GROUPS = 8
SG_GROUP = SG_HALF // SG_GROUPS

ML_WIDTH = D_MODEL
ML_HEADS = 8
ML_HEAD = ML_WIDTH // ML_HEADS
ML_QKV_BLOCK = 4
ML_CHUNK = CHUNK

N_EXPERTS = 32
TOP_K = 4
D_EXPERT = D_MODEL // 4
SWIGLU_LIMIT = 7.0
SWIGLU_ALPHA = 1.702

N_LAYERS_A = len(range(0, DEPTH, N_MIXERS))
N_LAYERS_B = len(range(1, DEPTH, N_MIXERS))
N_LAYERS_C = len(range(2, DEPTH, N_MIXERS))

kernel_name = "hybrid_rglru_sgu_mlstm_moe_deepnorm"

F32 = jnp.float32


def layer_norm(x, g, b):
    xf = x.astype(F32)
    mu = jnp.mean(xf, axis=-1, keepdims=True)
    var = jnp.mean(jnp.square(xf - mu), axis=-1, keepdims=True)
    return ((xf - mu) * lax.rsqrt(var + LN_EPS) * g.astype(F32) + b.astype(F32)).astype(x.dtype)


def causal_dwconv(x, w, b):
    ch = x.shape[-1]
    y = lax.conv_general_dilated(
        x, w[:, None, :].astype(x.dtype), window_strides=(1,), padding=[(CONV_W - 1, 0)],
        dimension_numbers=("NWC", "WIO", "NWC"), feature_group_count=ch)
    return y + b


def blockdiag(x, w):
    nb, bs, _ = w.shape
    xb = x.reshape(x.shape[:-1] + (nb, bs))
    return jnp.einsum("...nc,ncd->...nd", xb, w).reshape(x.shape)


def linear_recurrence(a, u):
    def combine(left, right):
        a_l, u_l = left
        a_r, u_r = right
        return a_l * a_r, a_r * u_l + u_r
    _, h = lax.associative_scan(combine, (a, u), axis=1)
    return h


def rglru_mixer(h, w_in, conv_w, conv_b, w_rgate, b_rgate, w_igate, b_igate, lam, w_out):
    gate_br, rec_br = jnp.split(h @ w_in, 2, axis=-1)
    xr = causal_dwconv(rec_br, conv_w, conv_b)
    r = jax.nn.sigmoid((blockdiag(xr, w_rgate) + b_rgate).astype(F32))
    i = jax.nn.sigmoid((blockdiag(xr, w_igate) + b_igate).astype(F32))
    log_a = -RG_C * r * jax.nn.softplus(-lam.astype(F32))
    u = jnp.sqrt(-jnp.expm1(2.0 * log_a)) * (i * xr.astype(F32))
    hs = linear_recurrence(jnp.exp(log_a), u)
    return (jax.nn.gelu(gate_br) * hs.astype(h.dtype)) @ w_out


def sgu_mixer(h, w_in, b_in, ln_g, ln_b, w_sp, b_sp, w_out, b_out):
    bsz, seq, _ = h.shape
    z = jax.nn.gelu(h @ w_in + b_in)
    u, v = jnp.split(z, 2, axis=-1)
    v = layer_norm(v, ln_g, ln_b)
    vb = v.reshape(bsz, seq // SG_BLOCK, SG_BLOCK, SG_GROUPS, SG_GROUP)
    pos = jnp.arange(SG_BLOCK)
    allowed = (pos[None, :] // CHUNK) <= (pos[:, None] // CHUNK)
    w = jnp.where(allowed[None], w_sp, jnp.zeros_like(w_sp))
    mixed = jnp.einsum("gij,bnjgc->bnigc", w, vb) + b_sp.T[None, None, :, :, None]
    return (u * mixed.reshape(bsz, seq, SG_HALF)) @ w_out + b_out


def mlstm_chunkwise(q, k, v, log_i, log_f):
    _, bsz, nh, L, d = q.shape
    causal = jnp.tril(jnp.ones((L, L), dtype=bool))

    def step(carry, xs):
        C, n, m = carry
        qc, kc, vc, li, lf = xs
        b = jnp.cumsum(lf, axis=-1)
        g = b + m[..., None]
        dmat = b[..., :, None] - b[..., None, :] + li[..., None, :]
        dmat = jnp.where(causal, dmat, -jnp.inf)
        m_t = jnp.maximum(g, jnp.max(dmat, axis=-1))
        w_intra = jnp.exp(dmat - m_t[..., None])
        w_inter = jnp.exp(g - m_t)
        s = jnp.einsum("bhtd,bhsd->bhts", qc, kc) * w_intra
        num = jnp.einsum("bhts,bhsv->bhtv", s, vc) + w_inter[..., None] * jnp.einsum("bhtk,bhkv->bhtv", qc, C)
        den = jnp.sum(s, axis=-1) + w_inter * jnp.einsum("bhtk,bhk->bht", qc, n)
        out = num / jnp.maximum(jnp.abs(den), jnp.exp(-m_t))[..., None]
        b_last = b[..., -1]
        dec = b_last[..., None] - b + li
        m_new = jnp.maximum(b_last + m, jnp.max(dec, axis=-1))
        ws = jnp.exp(dec - m_new[..., None])
        wc = jnp.exp(b_last + m - m_new)
        C_new = wc[..., None, None] * C + jnp.einsum("bhs,bhsk,bhsv->bhkv", ws, kc, vc)
        n_new = wc[..., None] * n + jnp.einsum("bhs,bhsk->bhk", ws, kc)
        return (C_new, n_new, m_new), out

    init = (jnp.zeros((bsz, nh, d, d), F32), jnp.zeros((bsz, nh, d), F32), jnp.zeros((bsz, nh), F32))
    _, hs = lax.scan(step, init, (q, k, v, log_i, log_f))
    return hs


def mlstm_mixer(h, w_in, conv_w, conv_b, w_q, w_k, w_v, w_gates, b_gates, skip, norm_g, w_out):
    bsz, seq, _ = h.shape
    nc = seq // ML_CHUNK
    xm, z = jnp.split(h @ w_in, 2, axis=-1)
    xc = jax.nn.silu(causal_dwconv(xm, conv_w, conv_b))
    q = blockdiag(xc, w_q)
    k = blockdiag(xc, w_k)
    v = blockdiag(xm, w_v)
    gates = (jnp.concatenate([q, k, v], axis=-1) @ w_gates + b_gates).astype(F32)
    log_i = gates[..., :ML_HEADS]
    log_f = jax.nn.log_sigmoid(gates[..., ML_HEADS:])

    def to_chunks(t):
        return t.astype(F32).reshape(bsz, nc, ML_CHUNK, ML_HEADS, ML_HEAD).transpose(1, 0, 3, 2, 4)

    def gate_chunks(t):
        return t.reshape(bsz, nc, ML_CHUNK, ML_HEADS).transpose(1, 0, 3, 2)

    hs = mlstm_chunkwise(to_chunks(q), to_chunks(k) * (ML_HEAD ** -0.5), to_chunks(v),
                         gate_chunks(log_i), gate_chunks(log_f))
    hs = hs.transpose(1, 0, 3, 2, 4).reshape(bsz, seq, ML_HEADS, ML_HEAD)
    mu = jnp.mean(hs, axis=-1, keepdims=True)
    var = jnp.mean(jnp.square(hs - mu), axis=-1, keepdims=True)
    hn = ((hs - mu) * lax.rsqrt(var + LN_EPS)).reshape(bsz, seq, ML_WIDTH) * norm_g.astype(F32)
    hn = hn.astype(h.dtype) + skip * xc
    return (jax.nn.sigmoid(z) * hn) @ w_out


def moe_ffn(h, w_router, b_router, w_gate_up, b_gate_up, w_down, b_down):
    bsz, seq, dm = h.shape
    t = h.reshape(bsz * seq, dm)
    logits = (t @ w_router + b_router).astype(F32)
    top_val, top_idx = lax.top_k(logits, TOP_K)
    probs = jax.nn.softmax(top_val, axis=-1)
    combine = jnp.sum(jax.nn.one_hot(top_idx, N_EXPERTS, dtype=F32) * probs[..., None], axis=1)
    out = jnp.zeros((bsz * seq, dm), F32)
    for e in range(N_EXPERTS):
        gate, up = jnp.split(t @ w_gate_up[e] + b_gate_up[e], 2, axis=-1)
        gate = jnp.minimum(gate, SWIGLU_LIMIT)
        up = jnp.clip(up, -SWIGLU_LIMIT, SWIGLU_LIMIT)
        act = gate * jax.nn.sigmoid(SWIGLU_ALPHA * gate) * (up + 1)
        out = out + combine[:, e:e + 1] * (act @ w_down[e] + b_down[e]).astype(F32)
    return out.reshape(bsz, seq, dm).astype(h.dtype)


def setup_inputs(seed: int = 0) -> dict:
    key = jax.random.key(seed)
    keys = iter(jax.random.split(key, 64))
    D = D_MODEL
    H = ML_HEADS

    def nrm(shape, scale):
        return jax.random.normal(next(keys), shape, F32) * scale

    x = nrm((BATCH, SEQ, D), 1.0)
    c = nrm((BATCH, D), 1.0)
    ada_w = nrm((DEPTH, D, 6 * D), 0.1 * D ** -0.5)
    ada_b = nrm((DEPTH, 6 * D), 0.02)
    ln1_g = 1.0 + nrm((DEPTH, D), 0.02)
    ln1_b = nrm((DEPTH, D), 0.02)
    ln2_g = 1.0 + nrm((DEPTH, D), 0.02)
    ln2_b = nrm((DEPTH, D), 0.02)

    rg_w_in = nrm((N_LAYERS_A, D, 2 * RG_WIDTH), D ** -0.5)
    rg_conv_w = nrm((N_LAYERS_A, CONV_W, RG_WIDTH), CONV_W ** -0.5)
    rg_conv_b = nrm((N_LAYERS_A, RG_WIDTH), 0.02)
    rg_w_rgate = nrm((N_LAYERS_A, RG_BLOCKS, RG_BLOCK, RG_BLOCK), RG_BLOCK ** -0.5)
    rg_b_rgate = nrm((N_LAYERS_A, RG_WIDTH), 0.02)
    rg_w_igate = nrm((N_LAYERS_A, RG_BLOCKS, RG_BLOCK, RG_BLOCK), RG_BLOCK ** -0.5)
    rg_b_igate = nrm((N_LAYERS_A, RG_WIDTH), 0.02)
    a_pow = jax.random.uniform(next(keys), (N_LAYERS_A, RG_WIDTH), F32, 0.9, 0.999)
    s = a_pow ** (1.0 / RG_C)
    rg_lam = jnp.log(s) - jnp.log1p(-s)
    rg_w_out = nrm((N_LAYERS_A, RG_WIDTH, D), DN_BETA * RG_WIDTH ** -0.5)

    sg_w_in = nrm((N_LAYERS_B, D, 2 * SG_HALF), D ** -0.5)
    sg_b_in = nrm((N_LAYERS_B, 2 * SG_HALF), 0.02)
    sg_ln_g = 1.0 + nrm((N_LAYERS_B, SG_HALF), 0.02)
    sg_ln_b = nrm((N_LAYERS_B, SG_HALF), 0.02)
    sg_w_sp = nrm((N_LAYERS_B, SG_GROUPS, SG_BLOCK, SG_BLOCK), 0.5 * SG_BLOCK ** -0.5)
    sg_b_sp = 1.0 + nrm((N_LAYERS_B, SG_GROUPS, SG_BLOCK), 0.02)
    sg_w_out = nrm((N_LAYERS_B, SG_HALF, D), DN_BETA * SG_HALF ** -0.5)
    sg_b_out = nrm((N_LAYERS_B, D), 0.02)

    nqb = ML_WIDTH // ML_QKV_BLOCK
    ml_w_in = nrm((N_LAYERS_C, D, 2 * ML_WIDTH), D ** -0.5)
    ml_conv_w = nrm((N_LAYERS_C, CONV_W, ML_WIDTH), CONV_W ** -0.5)
    ml_conv_b = nrm((N_LAYERS_C, ML_WIDTH), 0.02)
    ml_w_q = nrm((N_LAYERS_C, nqb, ML_QKV_BLOCK, ML_QKV_BLOCK), ML_QKV_BLOCK ** -0.5)
    ml_w_k = nrm((N_LAYERS_C, nqb, ML_QKV_BLOCK, ML_QKV_BLOCK), ML_QKV_BLOCK ** -0.5)
    ml_w_v = nrm((N_LAYERS_C, nqb, ML_QKV_BLOCK, ML_QKV_BLOCK), ML_QKV_BLOCK ** -0.5)
    ml_w_gates = nrm((N_LAYERS_C, 3 * ML_WIDTH, 2 * H), (3 * ML_WIDTH) ** -0.5)
    f_bias = jnp.broadcast_to(jnp.linspace(3.0, 6.0, H, dtype=F32), (N_LAYERS_C, H))
    ml_b_gates = jnp.concatenate([nrm((N_LAYERS_C, H), 0.1), f_bias + nrm((N_LAYERS_C, H), 0.02)], axis=-1)
    ml_skip = 1.0 + nrm((N_LAYERS_C, ML_WIDTH), 0.02)
    ml_norm_g = 1.0 + nrm((N_LAYERS_C, ML_WIDTH), 0.02)
    ml_w_out = nrm((N_LAYERS_C, ML_WIDTH, D), DN_BETA * ML_WIDTH ** -0.5)

    moe_w_router = nrm((DEPTH, D, N_EXPERTS), D ** -0.5)
    moe_b_router = nrm((DEPTH, N_EXPERTS), 0.01)
    moe_w_gate_up = nrm((DEPTH, N_EXPERTS, D, 2 * D_EXPERT), D ** -0.5)
    moe_b_gate_up = nrm((DEPTH, N_EXPERTS, 2 * D_EXPERT), 0.02)
    moe_w_down = nrm((DEPTH, N_EXPERTS, D_EXPERT, D), DN_BETA * D_EXPERT ** -0.5)
    moe_b_down = nrm((DEPTH, N_EXPERTS, D), 0.02)

    return {
        "x": x, "c": c, "ada_w": ada_w, "ada_b": ada_b,
        "ln1_g": ln1_g, "ln1_b": ln1_b, "ln2_g": ln2_g, "ln2_b": ln2_b,
        "rg_w_in": rg_w_in, "rg_conv_w": rg_conv_w, "rg_conv_b": rg_conv_b,
        "rg_w_rgate": rg_w_rgate, "rg_b_rgate": rg_b_rgate, "rg_w_igate": rg_w_igate,
        "rg_b_igate": rg_b_igate, "rg_lam": rg_lam, "rg_w_out": rg_w_out,
        "sg_w_in": sg_w_in, "sg_b_in": sg_b_in, "sg_ln_g": sg_ln_g, "sg_ln_b": sg_ln_b,
        "sg_w_sp": sg_w_sp, "sg_b_sp": sg_b_sp, "sg_w_out": sg_w_out, "sg_b_out": sg_b_out,
        "ml_w_in": ml_w_in, "ml_conv_w": ml_conv_w, "ml_conv_b": ml_conv_b,
        "ml_w_q": ml_w_q, "ml_w_k": ml_w_k, "ml_w_v": ml_w_v, "ml_w_gates": ml_w_gates,
        "ml_b_gates": ml_b_gates, "ml_skip": ml_skip, "ml_norm_g": ml_norm_g, "ml_w_out": ml_w_out,
        "moe_w_router": moe_w_router, "moe_b_router": moe_b_router,
        "moe_w_gate_up": moe_w_gate_up, "moe_b_gate_up": moe_b_gate_up,
        "moe_w_down": moe_w_down, "moe_b_down": moe_b_down,
    }


def reference(x, c, ada_w, ada_b, ln1_g, ln1_b, ln2_g, ln2_b,
              rg_w_in, rg_conv_w, rg_conv_b, rg_w_rgate, rg_b_rgate, rg_w_igate, rg_b_igate, rg_lam, rg_w_out,
              sg_w_in, sg_b_in, sg_ln_g, sg_ln_b, sg_w_sp, sg_b_sp, sg_w_out, sg_b_out,
              ml_w_in, ml_conv_w, ml_conv_b, ml_w_q, ml_w_k, ml_w_v, ml_w_gates, ml_b_gates, ml_skip,
              ml_norm_g, ml_w_out,
              moe_w_router, moe_b_router, moe_w_gate_up, moe_b_gate_up, moe_w_down, moe_b_down):
    cond = jax.nn.silu(c)
    for i in range(DEPTH):
        mod = (cond @ ada_w[i] + ada_b[i])[:, None, :]
        sh1, sc1, g1, sh2, sc2, g2 = jnp.split(mod, 6, axis=-1)
        h = x * (1 + sc1) + sh1
        kind, j = i % N_MIXERS, i // N_MIXERS
        if kind == 0:
            y = rglru_mixer(h, rg_w_in[j], rg_conv_w[j], rg_conv_b[j], rg_w_rgate[j], rg_b_rgate[j],
                            rg_w_igate[j], rg_b_igate[j], rg_lam[j], rg_w_out[j])
        elif kind == 1:
            y = sgu_mixer(h, sg_w_in[j], sg_b_in[j], sg_ln_g[j], sg_ln_b[j], sg_w_sp[j], sg_b_sp[j],
                          sg_w_out[j], sg_b_out[j])
        else:
            y = mlstm_mixer(h, ml_w_in[j], ml_conv_w[j], ml_conv_b[j], ml_w_q[j], ml_w_k[j], ml_w_v[j],
                            ml_w_gates[j], ml_b_gates[j], ml_skip[j], ml_norm_g[j], ml_w_out[j])
        x = layer_norm(DN_ALPHA * x + (1 + g1) * y, ln1_g[i], ln1_b[i])
        h = x * (1 + sc2) + sh2
        y = moe_ffn(h, moe_w_router[i], moe_b_router[i], moe_w_gate_up[i], moe_b_gate_up[i],
                    moe_w_down[i], moe_b_down[i])
        x = layer_norm(DN_ALPHA * x + (1 + g2) * y, ln2_g[i], ln2_b[i])
    return x
```

```python
import functools
import math

import jax
import jax.numpy as jnp
from jax import lax
from jax.experimental import pallas as pl
from jax.experimental.pallas import tpu as pltpu

F32 = jnp.float32
BF16 = jnp.bfloat16

LN_EPS = 1e-5
RG_C = 8.0
CONV_W = 4
CHUNK = 64
SG_BLOCK = 128
SG_GROUPS = 8
ML_HEADS = 8
TOP_K = 4
SWIGLU_LIMIT = 7.0
SWIGLU_ALPHA = 1.702

SUBLANES = 8
RG_SUPER = 640
ML_L = 256
EXPERT_TILE = 256
VMEM_LIMIT = 56 * 1024 * 1024
NEG = -1e30


def _cparams(sem):
    return pltpu.CompilerParams(dimension_semantics=sem, vmem_limit_bytes=VMEM_LIMIT)


def _sigmoid(x):
    return 1.0 / (1.0 + jnp.exp(-x))


def _gelu_tanh(x):
    return 0.5 * x * (1.0 + jnp.tanh(math.sqrt(2.0 / math.pi) * (x + 0.044715 * (x * x * x))))


def _softplus(x):
    return jnp.maximum(x, 0.0) + jnp.log1p(jnp.exp(-jnp.abs(x)))


def _layer_norm_rows(v, g, b):
    mu = jnp.mean(v, axis=-1, keepdims=True)
    d = v - mu
    var = jnp.mean(d * d, axis=-1, keepdims=True)
    return d * lax.rsqrt(var + LN_EPS) * g + b


def _const_spec(shape):
    nd = len(shape)
    return pl.BlockSpec(shape, lambda *_: (0,) * nd, pipeline_mode=pl.Buffered(1))


def _ada_kernel(c_ref, w_ref, b_ref, o_ref):
    c = c_ref[...]
    cond = c * _sigmoid(c)
    o_ref[...] = jnp.dot(cond.astype(BF16), w_ref[...].astype(BF16),
                         preferred_element_type=F32) + b_ref[...]


def ada_mod(c, ada_w, ada_b, tn=1024):
    depth, d, n = ada_w.shape
    bsz = c.shape[0]
    return pl.pallas_call(
        _ada_kernel,
        grid=(depth, n // tn),
        in_specs=[
            pl.BlockSpec((bsz, d), lambda l, j: (0, 0)),
            pl.BlockSpec((None, d, tn), lambda l, j: (l, 0, j)),
            pl.BlockSpec((None, 1, tn), lambda l, j: (l, 0, j)),
        ],
        out_specs=pl.BlockSpec((None, bsz, tn), lambda l, j: (l, 0, j)),
        out_shape=jax.ShapeDtypeStruct((depth, bsz, n), F32),
        compiler_params=_cparams(("parallel", "parallel")),
        name="ada_mod",
    )(c, ada_w, ada_b.reshape(depth, 1, n))


def _mm_in_kernel(x_ref, sc_ref, sh_ref, w_ref, o_ref, h_scr):
    @pl.when(pl.program_id(1) == 0)
    def _():
        h_scr[...] = (x_ref[...] * (1.0 + sc_ref[...]) + sh_ref[...]).astype(BF16)

    o_ref[...] = jnp.dot(h_scr[...], w_ref[...], preferred_element_type=F32).astype(o_ref.dtype)


def mm_in(x, mod4, k_sc, k_sh, w, seq, tm=1024, tn=512):
    t, d = x.shape
    n = w.shape[1]
    tm = min(tm, seq)
    rows_per_b = seq // tm
    return pl.pallas_call(
        _mm_in_kernel,
        grid=(t // tm, n // tn),
        in_specs=[
            pl.BlockSpec((tm, d), lambda i, j: (i, 0)),
            pl.BlockSpec((None, None, 1, d), lambda i, j: (i // rows_per_b, k_sc, 0, 0)),
            pl.BlockSpec((None, None, 1, d), lambda i, j: (i // rows_per_b, k_sh, 0, 0)),
            pl.BlockSpec((d, tn), lambda i, j: (0, j)),
        ],
        out_specs=pl.BlockSpec((tm, tn), lambda i, j: (i, j)),
        out_shape=jax.ShapeDtypeStruct((t, n), BF16),
        scratch_shapes=[pltpu.VMEM((tm, d), BF16)],
        compiler_params=_cparams(("parallel", "arbitrary")),
        name="mm_in",
    )(x, mod4, mod4, w)


def _mm_res_ln_kernel(a_ref, w_ref, b_ref, x_ref, g_ref, lng_ref, lnb_ref, sc_ref, sh_ref,
                      xo_ref, ho_ref, *, alpha):
    y = jnp.dot(a_ref[...], w_ref[...], preferred_element_type=F32) + b_ref[...]
    v = alpha * x_ref[...] + (1.0 + g_ref[...]) * y
    xn = _layer_norm_rows(v, lng_ref[...], lnb_ref[...])
    xo_ref[...] = xn
    ho_ref[...] = (xn * (1.0 + sc_ref[...]) + sh_ref[...]).astype(BF16)


def mm_res_ln(a, w, bias, x, mod4, k_gate, k_sc, k_sh, ln_g, ln_b, seq, alpha, tm=256):
    t, d = x.shape
    k = a.shape[1]
    rows_per_b = seq // tm
    modspec = lambda kk: pl.BlockSpec((None, None, 1, d), lambda i: (i // rows_per_b, kk, 0, 0))
    return pl.pallas_call(
        functools.partial(_mm_res_ln_kernel, alpha=alpha),
        grid=(t // tm,),
        in_specs=[
            pl.BlockSpec((tm, k), lambda i: (i, 0)),
            _const_spec((k, d)),
            _const_spec((1, d)),
            pl.BlockSpec((tm, d), lambda i: (i, 0)),
            modspec(k_gate),
            _const_spec((1, d)),
            _const_spec((1, d)),
            modspec(k_sc),
            modspec(k_sh),
        ],
        out_specs=[pl.BlockSpec((tm, d), lambda i: (i, 0)), pl.BlockSpec((tm, d), lambda i: (i, 0))],
        out_shape=[jax.ShapeDtypeStruct((t, d), F32), jax.ShapeDtypeStruct((t, d), BF16)],
        compiler_params=_cparams(("parallel",)),
        name="mm_res_ln",
    )(a, w, bias.reshape(1, d), x, mod4, ln_g.reshape(1, d), ln_b.reshape(1, d), mod4, mod4)


def _causal_conv(buf_ref, x, w_ref, b_ref, first):
    ts = x.shape[0]

    @pl.when(first)
    def _():
        buf_ref[0:SUBLANES, :] = jnp.zeros((SUBLANES, x.shape[1]), F32)

    buf_ref[SUBLANES:SUBLANES + ts, :] = x
    y = b_ref[...] + w_ref[CONV_W - 1:CONV_W, :] * x
    for back in range(1, CONV_W):
        y = y + w_ref[CONV_W - 1 - back:CONV_W - back, :] * buf_ref[SUBLANES - back:SUBLANES - back + ts, :]
    buf_ref[0:SUBLANES, :] = buf_ref[ts:ts + SUBLANES, :]
    return y


def _scan_rows(a, u, h0):
    ts, c = a.shape
    groups = ts // SUBLANES
    a3 = a.reshape(groups, SUBLANES, c)
    u3 = u.reshape(groups, SUBLANES, c)
    rid = lax.broadcasted_iota(jnp.int32, a3.shape, 1)
    for k in (1, 2, 4):
        keep = rid >= k
        a_sh = jnp.where(keep, pltpu.roll(a3, k, 1), 1.0)
        u_sh = jnp.where(keep, pltpu.roll(u3, k, 1), 0.0)
        u3 = u3 + a3 * u_sh
        a3 = a3 * a_sh
    h = h0
    outs = []
    for j in range(groups):
        hj = u3[j] + a3[j] * h
        outs.append(hj)
        h = hj[SUBLANES - 1:SUBLANES, :]
    return jnp.concatenate(outs, axis=0), h


def _rg_core_kernel(gate_ref, rec_ref, cw_ref, cb_ref, wri_ref, br_ref, bi_ref, lam_ref,
                    o_ref, buf_ref, h_ref):
    first = pl.program_id(2) == 0
    c = o_ref.shape[1]
    xr = _causal_conv(buf_ref, rec_ref[...].astype(F32), cw_ref, cb_ref, first)
    pre = jnp.dot(xr.astype(BF16), wri_ref[...], preferred_element_type=F32)
    r = _sigmoid(pre[:, :c] + br_ref[...])
    i = _sigmoid(pre[:, c:] + bi_ref[...])
    log_a = (-RG_C) * r * _softplus(-lam_ref[...])
    a = jnp.exp(log_a)
    u = jnp.sqrt(-jnp.tanh(log_a) * (a * a + 1.0)) * (i * xr)

    @pl.when(first)
    def _():
        h_ref[...] = jnp.zeros_like(h_ref)

    hs, h_last = _scan_rows(a, u, h_ref[...])
    h_ref[...] = h_last
    o_ref[...] = (_gelu_tanh(gate_ref[...].astype(F32)) * hs).astype(BF16)


def rg_core(z, conv_w, conv_b, w_ri, b_r, b_i, lam, bsz, seq, ts=256):
    t = z.shape[0]
    width = conv_w.shape[1]
    nsup = width // RG_SUPER
    spb = seq // ts
    row = lambda c, b, s: b * spb + s
    vec = lambda: pl.BlockSpec((1, RG_SUPER), lambda c, b, s: (0, c))
    return pl.pallas_call(
        _rg_core_kernel,
        grid=(nsup, bsz, spb),
        in_specs=[
            pl.BlockSpec((ts, RG_SUPER), lambda c, b, s: (row(c, b, s), c)),
            pl.BlockSpec((ts, RG_SUPER), lambda c, b, s: (row(c, b, s), nsup + c)),
            pl.BlockSpec((CONV_W, RG_SUPER), lambda c, b, s: (0, c)),
            vec(),
            pl.BlockSpec((None, RG_SUPER, 2 * RG_SUPER), lambda c, b, s: (c, 0, 0)),
            vec(), vec(), vec(),
        ],
        out_specs=pl.BlockSpec((ts, RG_SUPER), lambda c, b, s: (row(c, b, s), c)),
        out_shape=jax.ShapeDtypeStruct((t, width), BF16),
        scratch_shapes=[pltpu.VMEM((ts + SUBLANES, RG_SUPER), F32), pltpu.VMEM((1, RG_SUPER), F32)],
        compiler_params=_cparams(("parallel", "parallel", "arbitrary")),
        name="rg_core",
    )(z, z, conv_w, conv_b.reshape(1, width), w_ri, b_r.reshape(1, width), b_i.reshape(1, width),
      lam.reshape(1, width))


def _rg_gate_weights(w_r, w_i):
    nb, bs, _ = w_r.shape
    per = RG_SUPER // bs
    nsup = nb // per

    def dense(w):
        w4 = w.reshape(nsup, per, bs, bs)
        eye = jnp.eye(per, dtype=w.dtype)
        return jnp.einsum("spcd,pq->spcqd", w4, eye).reshape(nsup, RG_SUPER, RG_SUPER)

    return jnp.concatenate([dense(w_r), dense(w_i)], axis=-1).astype(BF16)


def _sg_in_kernel(x_ref, sc_ref, sh_ref, w_ref, b_ref, lng_ref, lnb_ref, u_ref, v_ref, h_scr):
    j = pl.program_id(1)

    @pl.when(j == 0)
    def _():
        h_scr[...] = (x_ref[...] * (1.0 + sc_ref[...]) + sh_ref[...]).astype(BF16)

    z = _gelu_tanh(jnp.dot(h_scr[...], w_ref[...], preferred_element_type=F32) + b_ref[...])

    @pl.when(j == 0)
    def _():
        u_ref[...] = z.astype(BF16)

    @pl.when(j == 1)
    def _():
        v_ref[...] = _layer_norm_rows(z, lng_ref[...], lnb_ref[...]).astype(BF16)


def sg_in(x, mod4, w, b, ln_g, ln_b, seq, tm=512):
    t, d = x.shape
    half = w.shape[1] // 2
    rows_per_b = seq // tm
    modspec = lambda kk: pl.BlockSpec((None, None, 1, d), lambda i, j: (i // rows_per_b, kk, 0, 0))
    return pl.pallas_call(
        _sg_in_kernel,
        grid=(t // tm, 2),
        in_specs=[
            pl.BlockSpec((tm, d), lambda i, j: (i, 0)),
            modspec(1), modspec(0),
            pl.BlockSpec((d, half), lambda i, j: (0, j)),
            pl.BlockSpec((1, half), lambda i, j: (0, j)),
            _const_spec((1, half)), _const_spec((1, half)),
        ],
        out_specs=[pl.BlockSpec((tm, half), lambda i, j: (i, 0)),
                   pl.BlockSpec((tm, half), lambda i, j: (i, 0))],
        out_shape=[jax.ShapeDtypeStruct((t, half), BF16), jax.ShapeDtypeStruct((t, half), BF16)],
        scratch_shapes=[pltpu.VMEM((tm, d), BF16)],
        compiler_params=_cparams(("parallel", "arbitrary")),
        name="sg_in",
    )(x, mod4, mod4, w, b.reshape(1, 2 * half), ln_g.reshape(1, half), ln_b.reshape(1, half))


def _sg_mix_kernel(u_ref, v_ref, w_ref, bt_ref, o_ref):
    nblk = u_ref.shape[0] // SG_BLOCK
    gw = u_ref.shape[1] // SG_GROUPS
    ii = lax.broadcasted_iota(jnp.int32, (SG_BLOCK, SG_BLOCK), 0)
    jj = lax.broadcasted_iota(jnp.int32, (SG_BLOCK, SG_BLOCK), 1)
    allowed = (jj // CHUNK) <= (ii // CHUNK)
    for g in range(SG_GROUPS):
        w = jnp.where(allowed, w_ref[g], 0.0).astype(BF16)
        bias = bt_ref[:, g:g + 1]
        for n in range(nblk):
            rows = slice(n * SG_BLOCK, (n + 1) * SG_BLOCK)
            cols = slice(g * gw, (g + 1) * gw)
            mixed = jnp.dot(w, v_ref[rows, cols], preferred_element_type=F32) + bias
            o_ref[rows, cols] = (u_ref[rows, cols].astype(F32) * mixed).astype(BF16)


def sg_mix(u, v, w_sp, b_sp, tm=256):
    t, half = u.shape
    return pl.pallas_call(
        _sg_mix_kernel,
        grid=(t // tm,),
        in_specs=[
            pl.BlockSpec((tm, half), lambda i: (i, 0)),
            pl.BlockSpec((tm, half), lambda i: (i, 0)),
            _const_spec(w_sp.shape),
            _const_spec((SG_BLOCK, SG_GROUPS)),
        ],
        out_specs=pl.BlockSpec((tm, half), lambda i: (i, 0)),
        out_shape=jax.ShapeDtypeStruct((t, half), BF16),
        compiler_params=_cparams(("parallel",)),
        name="sg_mix",
    )(u, v, w_sp, b_sp.T)


def _ml_qkv_kernel(xm_ref, cw_ref, cb_ref, wq_ref, wk_ref, wv_ref, wg_ref, bg_ref,
                   q_ref, k_ref, v_ref, xc_ref, g_ref, buf_ref):
    first = pl.program_id(1) == 0
    width = xm_ref.shape[1]
    hd = width // ML_HEADS
    xm = xm_ref[...]
    conv = _causal_conv(buf_ref, xm.astype(F32), cw_ref, cb_ref, first)
    xc = (conv * _sigmoid(conv)).astype(BF16)
    xc_ref[...] = xc
    for h in range(ML_HEADS):
        cols = slice(h * hd, (h + 1) * hd)
        q_ref[:, cols] = jnp.dot(xc[:, cols], wq_ref[h], preferred_element_type=F32).astype(BF16)
        k_ref[:, cols] = jnp.dot(xc[:, cols], wk_ref[h], preferred_element_type=F32).astype(BF16)
        v_ref[:, cols] = jnp.dot(xm[:, cols], wv_ref[h], preferred_element_type=F32).astype(BF16)
    gates = (jnp.dot(q_ref[...], wg_ref[0], preferred_element_type=F32)
             + jnp.dot(k_ref[...], wg_ref[1], preferred_element_type=F32)
             + jnp.dot(v_ref[...], wg_ref[2], preferred_element_type=F32) + bg_ref[...])
    is_f = lax.broadcasted_iota(jnp.int32, gates.shape, 1) >= ML_HEADS
    g_ref[...] = jnp.where(is_f, -_softplus(-gates), gates)


def _ml_headwise_weights(w):
    nqb, bs, _ = w.shape
    per = nqb // ML_HEADS
    w4 = w.reshape(ML_HEADS, per, bs, bs)
    eye = jnp.eye(per, dtype=w.dtype)
    return jnp.einsum("hpcd,pq->hpcqd", w4, eye).reshape(ML_HEADS, per * bs, per * bs).astype(BF16)


def ml_qkv(zz, conv_w, conv_b, wq, wk, wv, w_gates, b_gates, bsz, seq, ts=256):
    t = zz.shape[0]
    width = conv_w.shape[1]
    hd = width // ML_HEADS
    spb = seq // ts
    row = lambda b, s: (b * spb + s, 0)
    act = lambda: pl.BlockSpec((ts, width), row)
    return pl.pallas_call(
        _ml_qkv_kernel,
        grid=(bsz, spb),
        in_specs=[
            act(),
            _const_spec((CONV_W, width)), _const_spec((1, width)),
            _const_spec((ML_HEADS, hd, hd)), _const_spec((ML_HEADS, hd, hd)), _const_spec((ML_HEADS, hd, hd)),
            _const_spec((3, width, 2 * ML_HEADS)), _const_spec((1, 2 * ML_HEADS)),
        ],
        out_specs=[act(), act(), act(), act(), pl.BlockSpec((ts, 2 * ML_HEADS), row)],
        out_shape=[jax.ShapeDtypeStruct((t, width), BF16)] * 4
                  + [jax.ShapeDtypeStruct((t, 2 * ML_HEADS), F32)],
        scratch_shapes=[pltpu.VMEM((ts + SUBLANES, width), F32)],
        compiler_params=_cparams(("parallel", "arbitrary")),
        name="ml_qkv",
    )(zz, conv_w, conv_b.reshape(1, width), wq, wk, wv,
      w_gates.reshape(3, width, 2 * ML_HEADS).astype(BF16), b_gates.reshape(1, 2 * ML_HEADS))


def _ml_chunk_kernel(q_ref, k_ref, v_ref, xc_ref, z_ref, gc_ref, gr_ref, ng_ref, skip_ref,
                     o_ref, c_scr, n_scr, m_scr):
    L, width = q_ref.shape
    hd = width // ML_HEADS
    scale = hd ** -0.5

    @pl.when(pl.program_id(1) == 0)
    def _():
        c_scr[...] = jnp.zeros_like(c_scr)
        n_scr[...] = jnp.zeros_like(n_scr)
        m_scr[...] = jnp.zeros_like(m_scr)

    ri = lax.broadcasted_iota(jnp.int32, (L, L), 0)
    ci = lax.broadcasted_iota(jnp.int32, (L, L), 1)
    causal = ri >= ci
    lower = causal.astype(F32)
    upper = (ri <= ci).astype(F32)
    gc = gc_ref[...]
    gr = gr_ref[...]
    bcol = jnp.dot(lower, gc[:, ML_HEADS:], preferred_element_type=F32, precision=lax.Precision.HIGHEST)
    brow = jnp.dot(gr[ML_HEADS:, :], upper, preferred_element_type=F32, precision=lax.Precision.HIGHEST)

    for h in range(ML_HEADS):
        cols = slice(h * hd, (h + 1) * hd)
        bc = bcol[:, h:h + 1]
        br = brow[h:h + 1, :]
        lir = gr[h:h + 1, :]
        lic = gc[:, h:h + 1]
        m_prev = m_scr[h:h + 1, 0:1]
        g = bc + m_prev
        dmat = jnp.where(causal, bc - br + lir, NEG)
        m_t = jnp.maximum(g, jnp.max(dmat, axis=-1, keepdims=True))
        w_intra = jnp.exp(dmat - m_t)
        w_inter = jnp.exp(g - m_t)
        qh = q_ref[:, cols]
        kh = k_ref[:, cols]
        vh = v_ref[:, cols]
        s = lax.dot_general(qh, kh, (((1,), (1,)), ((), ())), preferred_element_type=F32) * scale * w_intra
        c_old = c_scr[h]
        n_old = n_scr[h:h + 1, :]
        num = (jnp.dot(s.astype(BF16), vh, preferred_element_type=F32)
               + w_inter * jnp.dot(qh, c_old.astype(BF16), preferred_element_type=F32))
        den = (jnp.sum(s, axis=-1, keepdims=True)
               + w_inter * jnp.sum(qh.astype(F32) * n_old, axis=-1, keepdims=True))
        out = num / jnp.maximum(jnp.abs(den), jnp.exp(-m_t))

        b_last = bc[L - 1:L, :]
        dec = b_last - bc + lic
        m_new = jnp.maximum(b_last + m_prev, jnp.max(dec, axis=0, keepdims=True))
        ws = jnp.exp(dec - m_new)
        wc = jnp.exp(b_last + m_prev - m_new)
        kw = kh.astype(F32) * (ws * scale)
        c_scr[h] = wc * c_old + lax.dot_general(kw.astype(BF16), vh, (((0,), (0,)), ((), ())),
                                                preferred_element_type=F32)
        n_scr[h:h + 1, :] = wc * n_old + jnp.sum(kw, axis=0, keepdims=True)
        m_scr[h:h + 1, :] = jnp.broadcast_to(m_new, (1, m_scr.shape[1]))

        mu = jnp.mean(out, axis=-1, keepdims=True)
        dlt = out - mu
        var = jnp.mean(dlt * dlt, axis=-1, keepdims=True)
        hn = dlt * lax.rsqrt(var + LN_EPS) * ng_ref[:, cols]
        hn = hn + skip_ref[:, cols] * xc_ref[:, cols].astype(F32)
        o_ref[:, cols] = (_sigmoid(z_ref[:, cols].astype(F32)) * hn).astype(BF16)


def ml_chunk(q, k, v, xc, zz, gates, gates_t, norm_g, skip, bsz, seq):
    t, width = q.shape
    hd = width // ML_HEADS
    L = ML_L
    cpb = seq // L
    row = lambda b, c: (b * cpb + c, 0)
    act = lambda: pl.BlockSpec((L, width), row)
    return pl.pallas_call(
        _ml_chunk_kernel,
        grid=(bsz, cpb),
        in_specs=[
            act(), act(), act(), act(),
            pl.BlockSpec((L, width), lambda b, c: (b * cpb + c, 1)),
            pl.BlockSpec((L, 2 * ML_HEADS), row),
            pl.BlockSpec((2 * ML_HEADS, L), lambda b, c: (0, b * cpb + c)),
            _const_spec((1, width)), _const_spec((1, width)),
        ],
        out_specs=act(),
        out_shape=jax.ShapeDtypeStruct((t, width), BF16),
        scratch_shapes=[pltpu.VMEM((ML_HEADS, hd, hd), F32), pltpu.VMEM((ML_HEADS, hd), F32),
                        pltpu.VMEM((ML_HEADS, 128), F32)],
        compiler_params=_cparams(("parallel", "arbitrary")),
        name="ml_chunk",
    )(q, k, v, xc, zz, gates, gates_t, norm_g.reshape(1, width), skip.reshape(1, width))


def _router_kernel(h_ref, w_ref, b_ref, eid_ref, prob_ref, rank_ref, cnt_ref, cnt_scr):
    tr = h_ref.shape[0]
    ne = w_ref.shape[1]

    @pl.when(pl.program_id(0) == 0)
    def _():
        cnt_scr[...] = jnp.zeros_like(cnt_scr)

    logits = jnp.dot(h_ref[...], w_ref[...], preferred_element_type=F32) + b_ref[...]
    lane = lax.broadcasted_iota(jnp.int32, (tr, ne), 1)
    work = logits
    sel = jnp.zeros((tr, ne), F32)
    hits = []
    top = None
    for k in range(TOP_K):
        m = jnp.max(work, axis=-1, keepdims=True)
        idx = jnp.min(jnp.where(work == m, lane, ne), axis=-1, keepdims=True)
        hit = lane == idx
        hits.append((hit, idx))
        sel = jnp.where(hit, 1.0, sel)
        work = jnp.where(hit, -jnp.inf, work)
        if k == 0:
            top = m
    e = sel * jnp.exp(logits - top)
    p = e / jnp.sum(e, axis=-1, keepdims=True)

    ri = lax.broadcasted_iota(jnp.int32, (tr, tr), 0)
    ci = lax.broadcasted_iota(jnp.int32, (tr, tr), 1)
    strict_lower = (ri > ci).astype(BF16)
    before = jnp.dot(strict_lower, sel.astype(BF16), preferred_element_type=F32)
    rank = cnt_scr[...] + before
    cnt_scr[...] = cnt_scr[...] + jnp.sum(sel, axis=0, keepdims=True)
    cnt_ref[...] = cnt_scr[...].astype(jnp.int32)

    eid = jnp.zeros((tr, ne), jnp.int32)
    prob = jnp.zeros((tr, ne), F32)
    rnk = jnp.zeros((tr, ne), F32)
    for k, (hit, idx) in enumerate(hits):
        slot = lane == k
        eid = jnp.where(slot, idx, eid)
        prob = jnp.where(slot, jnp.sum(jnp.where(hit, p, 0.0), axis=-1, keepdims=True), prob)
        rnk = jnp.where(slot, jnp.sum(jnp.where(hit, rank, 0.0), axis=-1, keepdims=True), rnk)
    eid_ref[...] = eid
    prob_ref[...] = prob
    rank_ref[...] = rnk.astype(jnp.int32)


def router(h2, w_router, b_router, tr=256):
    t, d = h2.shape
    ne = w_router.shape[1]
    row = lambda i: (i, 0)
    return pl.pallas_call(
        _router_kernel,
        grid=(t // tr,),
        in_specs=[pl.BlockSpec((tr, d), row), _const_spec((d, ne)), _const_spec((1, ne))],
        out_specs=[pl.BlockSpec((tr, ne), row), pl.BlockSpec((tr, ne), row), pl.BlockSpec((tr, ne), row),
                   pl.BlockSpec((1, ne), lambda i: (0, 0))],
        out_shape=[jax.ShapeDtypeStruct((t, ne), jnp.int32), jax.ShapeDtypeStruct((t, ne), F32),
                   jax.ShapeDtypeStruct((t, ne), jnp.int32), jax.ShapeDtypeStruct((1, ne), jnp.int32)],
        scratch_shapes=[pltpu.VMEM((1, ne), F32)],
        compiler_params=_cparams(("arbitrary",)),
        name="router",
    )(h2, w_router.astype(BF16), b_router.reshape(1, ne))


def _expert_kernel(te_ref, nu_ref, x_ref, wgu_ref, bgu_ref, wd_ref, bd_ref, o_ref):
    used = pl.program_id(0) < nu_ref[0]

    @pl.when(used)
    def _():
        f = wd_ref.shape[0]
        gu = jnp.dot(x_ref[...], wgu_ref[...], preferred_element_type=F32) + bgu_ref[...]
        gate = jnp.minimum(gu[:, :f], SWIGLU_LIMIT)
        up = jnp.clip(gu[:, f:], -SWIGLU_LIMIT, SWIGLU_LIMIT)
        act = gate * _sigmoid(SWIGLU_ALPHA * gate) * (up + 1.0)
        y = jnp.dot(act.astype(BF16), wd_ref[...], preferred_element_type=F32) + bd_ref[...]
        o_ref[...] = y.astype(BF16)

    @pl.when(jnp.logical_not(used))
    def _():
        o_ref[...] = jnp.zeros_like(o_ref)


def experts(xs, tile_e, n_used, w_gu, b_gu, w_d, b_d):
    p, d = xs.shape
    ne, _, f2 = w_gu.shape
    f = f2 // 2
    te = EXPERT_TILE
    grid_spec = pltpu.PrefetchScalarGridSpec(
        num_scalar_prefetch=2,
        grid=(p // te,),
        in_specs=[
            pl.BlockSpec((te, d), lambda i, e, n: (i, 0)),
            pl.BlockSpec((None, d, f2), lambda i, e, n: (e[i], 0, 0)),
            pl.BlockSpec((None, 1, f2), lambda i, e, n: (e[i], 0, 0)),
            pl.BlockSpec((None, f, d), lambda i, e, n: (e[i], 0, 0)),
            pl.BlockSpec((None, 1, d), lambda i, e, n: (e[i], 0, 0)),
        ],
        out_specs=pl.BlockSpec((te, d), lambda i, e, n: (i, 0)),
    )
    return pl.pallas_call(
        _expert_kernel,
        grid_spec=grid_spec,
        out_shape=jax.ShapeDtypeStruct((p, d), BF16),
        compiler_params=_cparams(("arbitrary",)),
        name="experts",
    )(tile_e, n_used, xs, w_gu, b_gu.reshape(ne, 1, f2), w_d, b_d.reshape(ne, 1, d))


def _combine_ln_kernel(y0_ref, y1_ref, y2_ref, y3_ref, p_ref, x_ref, g_ref, lng_ref, lnb_ref, xo_ref,
                       *, alpha):
    p = p_ref[...]
    y = (p[:, 0:1] * y0_ref[...].astype(F32) + p[:, 1:2] * y1_ref[...].astype(F32)
         + p[:, 2:3] * y2_ref[...].astype(F32) + p[:, 3:4] * y3_ref[...].astype(F32))
    v = alpha * x_ref[...] + (1.0 + g_ref[...]) * y
    xo_ref[...] = _layer_norm_rows(v, lng_ref[...], lnb_ref[...])


def combine_ln(ys, prob, x, mod4, ln_g, ln_b, seq, alpha, tm=256):
    t, d = x.shape
    ne = prob.shape[1]
    rows_per_b = seq // tm
    row = lambda i: (i, 0)
    return pl.pallas_call(
        functools.partial(_combine_ln_kernel, alpha=alpha),
        grid=(t // tm,),
        in_specs=[pl.BlockSpec((tm, d), row)] * 4 + [
            pl.BlockSpec((tm, ne), row),
            pl.BlockSpec((tm, d), row),
            pl.BlockSpec((None, None, 1, d), lambda i: (i // rows_per_b, 5, 0, 0)),
            _const_spec((1, d)), _const_spec((1, d)),
        ],
        out_specs=pl.BlockSpec((tm, d), row),
        out_shape=jax.ShapeDtypeStruct((t, d), F32),
        compiler_params=_cparams(("parallel",)),
        name="combine_ln",
    )(*ys, prob, x, mod4, ln_g.reshape(1, d), ln_b.reshape(1, d))


def moe_layer(h2, x, mod4, w_router, b_router, w_gu, b_gu, w_d, b_d, ln_g, ln_b, seq, alpha):
    t, d = h2.shape
    ne = w_router.shape[1]
    te = EXPERT_TILE
    eid, prob, rank, counts = router(h2, w_router, b_router)
    counts = counts[0]
    padded = ((counts + te - 1) // te) * te
    ends = jnp.cumsum(padded)
    starts = ends - padded
    eid4 = eid[:, :TOP_K]
    dest4 = jnp.take(starts, eid4) + rank[:, :TOP_K]
    n_slots = t * TOP_K + ne * te
    tok = jnp.broadcast_to(jnp.arange(t, dtype=jnp.int32)[:, None], (t, TOP_K))
    src = jnp.zeros((n_slots,), jnp.int32).at[dest4.reshape(-1)].set(tok.reshape(-1))
    n_tiles = n_slots // te
    tile_start = jnp.arange(n_tiles, dtype=jnp.int32) * te
    tile_e = jnp.minimum(jnp.searchsorted(ends, tile_start, side="right"), ne - 1).astype(jnp.int32)
    n_used = (ends[-1] // te).astype(jnp.int32).reshape(1)
    last_e = jnp.take(tile_e, jnp.maximum(n_used[0] - 1, 0))
    tile_e = jnp.where(tile_start < ends[-1], tile_e, last_e)

    xs = jnp.take(h2, src, axis=0)
    ysorted = experts(xs, tile_e, n_used, w_gu, b_gu, w_d, b_d)
    ys = [jnp.take(ysorted, dest4[:, k], axis=0) for k in range(TOP_K)]
    return combine_ln(ys, prob, x, mod4, ln_g, ln_b, seq, alpha)


def kernel(x, c, ada_w, ada_b, ln1_g, ln1_b, ln2_g, ln2_b, rg_w_in, rg_conv_w, rg_conv_b, rg_w_rgate, rg_b_rgate, rg_w_igate, rg_b_igate, rg_lam, rg_w_out, sg_w_in, sg_b_in, sg_ln_g, sg_ln_b, sg_w_sp, sg_b_sp, sg_w_out, sg_b_out, ml_w_in, ml_conv_w, ml_conv_b, ml_w_q, ml_w_k, ml_w_v, ml_w_gates, ml_b_gates, ml_skip, ml_norm_g, ml_w_out, moe_w_router, moe_b_router, moe_w_gate_up, moe_b_gate_up, moe_w_down, moe_b_down):
    bsz, seq, d = x.shape
    depth = ada_w.shape[0]
    alpha = (2 * depth) ** 0.25
    t = bsz * seq
    xt = x.reshape(t, d)
    mod = ada_mod(c, ada_w, ada_b)
    zero_bias = jnp.zeros((d,), F32)

    for i in range(depth):
        mod4 = mod[i].reshape(bsz, 6, 1, d)
        kind, j = i % 3, i // 3
        if kind == 0:
            z = mm_in(xt, mod4, 1, 0, rg_w_in[j].astype(BF16), seq)
            w_ri = _rg_gate_weights(rg_w_rgate[j], rg_w_igate[j])
            a = rg_core(z, rg_conv_w[j], rg_conv_b[j], w_ri, rg_b_rgate[j], rg_b_igate[j], rg_lam[j], bsz, seq)
            w_out, b_out = rg_w_out[j], zero_bias
        elif kind == 1:
            u, v = sg_in(xt, mod4, sg_w_in[j].astype(BF16), sg_b_in[j], sg_ln_g[j], sg_ln_b[j], seq)
            a = sg_mix(u, v, sg_w_sp[j], sg_b_sp[j])
            w_out, b_out = sg_w_out[j], sg_b_out[j]
        else:
            zz = mm_in(xt, mod4, 1, 0, ml_w_in[j].astype(BF16), seq)
            q, k, v, xc, gates = ml_qkv(zz, ml_conv_w[j], ml_conv_b[j], _ml_headwise_weights(ml_w_q[j]),
                                        _ml_headwise_weights(ml_w_k[j]), _ml_headwise_weights(ml_w_v[j]),
                                        ml_w_gates[j], ml_b_gates[j], bsz, seq)
            a = ml_chunk(q, k, v, xc, zz, gates, gates.T, ml_norm_g[j], ml_skip[j], bsz, seq)
            w_out, b_out = ml_w_out[j], zero_bias
        xt, h2 = mm_res_ln(a, w_out.astype(BF16), b_out, xt, mod4, 2, 4, 3, ln1_g[i], ln1_b[i], seq, alpha)
        xt = moe_layer(h2, xt, mod4, moe_w_router[i], moe_b_router[i], moe_w_gate_up[i].astype(BF16),
                       moe_b_gate_up[i], moe_w_down[i].astype(BF16), moe_b_down[i], ln2_g[i], ln2_b[i],
                       seq, alpha)
    return xt.reshape(bsz, seq, d)
```

```python
import functools
import math

import jax
import jax.numpy as jnp
from jax import lax
from jax.experimental import pallas as pl
from jax.experimental.pallas import tpu as pltpu

F32 = jnp.float32
BF16 = jnp.bfloat16

LN_EPS = 1e-5
RG_C = 8.0
CONV_W = 4
CHUNK = 64
SG_BLOCK = 128
SG_GROUPS = 8
ML_HEADS = 8
TOP_K = 4
SWIGLU_LIMIT = 7.0
SWIGLU_ALPHA = 1.702

SUBLANES = 8
RG_SUPER = 640
ML_L = 256
EXPERT_TILE = 512
VMEM_LIMIT = 56 * 1024 * 1024
NEG = -1e30
_IN_BOUNDS = "promise_in_bounds"


def _cparams(sem):
    return pltpu.CompilerParams(dimension_semantics=sem, vmem_limit_bytes=VMEM_LIMIT)


def _sigmoid(x):
    return 1.0 / (1.0 + jnp.exp(-x))


def _gelu_tanh(x):
    return 0.5 * x * (1.0 + jnp.tanh(math.sqrt(2.0 / math.pi) * (x + 0.044715 * (x * x * x))))


def _softplus(x):
    return jnp.maximum(x, 0.0) + jnp.log1p(jnp.exp(-jnp.abs(x)))


def _layer_norm_rows(v, g, b):
    mu = jnp.mean(v, axis=-1, keepdims=True)
    d = v - mu
    var = jnp.mean(d * d, axis=-1, keepdims=True)
    return d * lax.rsqrt(var + LN_EPS) * g + b


def _const_spec(shape):
    nd = len(shape)
    return pl.BlockSpec(shape, lambda *_: (0,) * nd, pipeline_mode=pl.Buffered(1))


def _ada_kernel(c_ref, w_ref, b_ref, o_ref):
    c = c_ref[...]
    cond = c * _sigmoid(c)
    o_ref[...] = jnp.dot(cond.astype(BF16), w_ref[...].astype(BF16),
                         preferred_element_type=F32) + b_ref[...]


def ada_mod(c, ada_w, ada_b, tn=1024):
    depth, d, n = ada_w.shape
    bsz = c.shape[0]
    return pl.pallas_call(
        _ada_kernel,
        grid=(depth, n // tn),
        in_specs=[
            pl.BlockSpec((bsz, d), lambda l, j: (0, 0)),
            pl.BlockSpec((None, d, tn), lambda l, j: (l, 0, j)),
            pl.BlockSpec((None, 1, tn), lambda l, j: (l, 0, j)),
        ],
        out_specs=pl.BlockSpec((None, bsz, tn), lambda l, j: (l, 0, j)),
        out_shape=jax.ShapeDtypeStruct((depth, bsz, n), F32),
        compiler_params=_cparams(("parallel", "parallel")),
        name="ada_mod",
    )(c, ada_w, ada_b.reshape(depth, 1, n))


def _mm_in_kernel(x_ref, sc_ref, sh_ref, w_ref, o_ref, h_scr):
    @pl.when(pl.program_id(1) == 0)
    def _():
        h_scr[...] = (x_ref[...] * (1.0 + sc_ref[...]) + sh_ref[...]).astype(BF16)

    o_ref[...] = jnp.dot(h_scr[...], w_ref[...], preferred_element_type=F32).astype(o_ref.dtype)


def mm_in(x, mod4, k_sc, k_sh, w, seq, tm=1024, tn=512):
    t, d = x.shape
    n = w.shape[1]
    tm = min(tm, seq)
    rows_per_b = seq // tm
    return pl.pallas_call(
        _mm_in_kernel,
        grid=(t // tm, n // tn),
        in_specs=[
            pl.BlockSpec((tm, d), lambda i, j: (i, 0)),
            pl.BlockSpec((None, None, 1, d), lambda i, j: (i // rows_per_b, k_sc, 0, 0)),
            pl.BlockSpec((None, None, 1, d), lambda i, j: (i // rows_per_b, k_sh, 0, 0)),
            pl.BlockSpec((d, tn), lambda i, j: (0, j)),
        ],
        out_specs=pl.BlockSpec((tm, tn), lambda i, j: (i, j)),
        out_shape=jax.ShapeDtypeStruct((t, n), BF16),
        scratch_shapes=[pltpu.VMEM((tm, d), BF16)],
        compiler_params=_cparams(("parallel", "arbitrary")),
        name="mm_in",
    )(x, mod4, mod4, w)


def _mm_res_ln_kernel(a_ref, w_ref, b_ref, x_ref, g_ref, lng_ref, lnb_ref, sc_ref, sh_ref,
                      xo_ref, ho_ref, *, alpha):
    y = jnp.dot(a_ref[...], w_ref[...], preferred_element_type=F32) + b_ref[...]
    v = alpha * x_ref[...] + (1.0 + g_ref[...]) * y
    xn = _layer_norm_rows(v, lng_ref[...], lnb_ref[...])
    xo_ref[...] = xn
    ho_ref[...] = (xn * (1.0 + sc_ref[...]) + sh_ref[...]).astype(BF16)


def mm_res_ln(a, w, bias, x, mod4, k_gate, k_sc, k_sh, ln_g, ln_b, seq, alpha, tm=256):
    t, d = x.shape
    k = a.shape[1]
    rows_per_b = seq // tm
    modspec = lambda kk: pl.BlockSpec((None, None, 1, d), lambda i: (i // rows_per_b, kk, 0, 0))
    return pl.pallas_call(
        functools.partial(_mm_res_ln_kernel, alpha=alpha),
        grid=(t // tm,),
        in_specs=[
            pl.BlockSpec((tm, k), lambda i: (i, 0)),
            _const_spec((k, d)),
            _const_spec((1, d)),
            pl.BlockSpec((tm, d), lambda i: (i, 0)),
            modspec(k_gate),
            _const_spec((1, d)),
            _const_spec((1, d)),
            modspec(k_sc),
            modspec(k_sh),
        ],
        out_specs=[pl.BlockSpec((tm, d), lambda i: (i, 0)), pl.BlockSpec((tm, d), lambda i: (i, 0))],
        out_shape=[jax.ShapeDtypeStruct((t, d), F32), jax.ShapeDtypeStruct((t, d), BF16)],
        compiler_params=_cparams(("parallel",)),
        name="mm_res_ln",
    )(a, w, bias.reshape(1, d), x, mod4, ln_g.reshape(1, d), ln_b.reshape(1, d), mod4, mod4)


def _causal_conv(buf_ref, x, w_ref, b_ref, first):
    ts = x.shape[0]

    @pl.when(first)
    def _():
        buf_ref[0:SUBLANES, :] = jnp.zeros((SUBLANES, x.shape[1]), F32)

    buf_ref[SUBLANES:SUBLANES + ts, :] = x
    y = b_ref[...] + w_ref[CONV_W - 1:CONV_W, :] * x
    for back in range(1, CONV_W):
        y = y + w_ref[CONV_W - 1 - back:CONV_W - back, :] * buf_ref[SUBLANES - back:SUBLANES - back + ts, :]
    buf_ref[0:SUBLANES, :] = buf_ref[ts:ts + SUBLANES, :]
    return y


def _scan_rows(a, u, h0):
    ts, c = a.shape
    groups = ts // SUBLANES
    a3 = a.reshape(groups, SUBLANES, c)
    u3 = u.reshape(groups, SUBLANES, c)
    rid = lax.broadcasted_iota(jnp.int32, a3.shape, 1)
    for k in (1, 2, 4):
        keep = rid >= k
        a_sh = jnp.where(keep, pltpu.roll(a3, k, 1), 1.0)
        u_sh = jnp.where(keep, pltpu.roll(u3, k, 1), 0.0)
        u3 = u3 + a3 * u_sh
        a3 = a3 * a_sh
    h = h0
    outs = []
    for j in range(groups):
        hj = u3[j] + a3[j] * h
        outs.append(hj)
        h = hj[SUBLANES - 1:SUBLANES, :]
    return jnp.concatenate(outs, axis=0), h


def _rg_core_kernel(gate_ref, rec_ref, cw_ref, cb_ref, wri_ref, br_ref, bi_ref, lam_ref,
                    o_ref, buf_ref, h_ref):
    first = pl.program_id(2) == 0
    c = o_ref.shape[1]
    xr = _causal_conv(buf_ref, rec_ref[...].astype(F32), cw_ref, cb_ref, first)
    pre = jnp.dot(xr.astype(BF16), wri_ref[...], preferred_element_type=F32)
    r = _sigmoid(pre[:, :c] + br_ref[...])
    i = _sigmoid(pre[:, c:] + bi_ref[...])
    log_a = (-RG_C) * r * _softplus(-lam_ref[...])
    a = jnp.exp(log_a)
    u = jnp.sqrt(-jnp.tanh(log_a) * (a * a + 1.0)) * (i * xr)

    @pl.when(first)
    def _():
        h_ref[...] = jnp.zeros_like(h_ref)

    hs, h_last = _scan_rows(a, u, h_ref[...])
    h_ref[...] = h_last
    o_ref[...] = (_gelu_tanh(gate_ref[...].astype(F32)) * hs).astype(BF16)


def rg_core(z, conv_w, conv_b, w_ri, b_r, b_i, lam, bsz, seq, ts=256):
    t = z.shape[0]
    width = conv_w.shape[1]
    nsup = width // RG_SUPER
    spb = seq // ts
    row = lambda c, b, s: b * spb + s
    vec = lambda: pl.BlockSpec((1, RG_SUPER), lambda c, b, s: (0, c))
    return pl.pallas_call(
        _rg_core_kernel,
        grid=(nsup, bsz, spb),
        in_specs=[
            pl.BlockSpec((ts, RG_SUPER), lambda c, b, s: (row(c, b, s), c)),
            pl.BlockSpec((ts, RG_SUPER), lambda c, b, s: (row(c, b, s), nsup + c)),
            pl.BlockSpec((CONV_W, RG_SUPER), lambda c, b, s: (0, c)),
            vec(),
            pl.BlockSpec((None, RG_SUPER, 2 * RG_SUPER), lambda c, b, s: (c, 0, 0)),
            vec(), vec(), vec(),
        ],
        out_specs=pl.BlockSpec((ts, RG_SUPER), lambda c, b, s: (row(c, b, s), c)),
        out_shape=jax.ShapeDtypeStruct((t, width), BF16),
        scratch_shapes=[pltpu.VMEM((ts + SUBLANES, RG_SUPER), F32), pltpu.VMEM((1, RG_SUPER), F32)],
        compiler_params=_cparams(("parallel", "parallel", "arbitrary")),
        name="rg_core",
    )(z, z, conv_w, conv_b.reshape(1, width), w_ri, b_r.reshape(1, width), b_i.reshape(1, width),
      lam.reshape(1, width))


def _rg_gate_weights(w_r, w_i):
    nb, bs, _ = w_r.shape
    per = RG_SUPER // bs
    nsup = nb // per

    def dense(w):
        w4 = w.reshape(nsup, per, bs, bs)
        eye = jnp.eye(per, dtype=w.dtype)
        return jnp.einsum("spcd,pq->spcqd", w4, eye).reshape(nsup, RG_SUPER, RG_SUPER)

    return jnp.concatenate([dense(w_r), dense(w_i)], axis=-1).astype(BF16)


def _sg_in_kernel(x_ref, sc_ref, sh_ref, w_ref, b_ref, lng_ref, lnb_ref, u_ref, v_ref, h_scr):
    j = pl.program_id(1)

    @pl.when(j == 0)
    def _():
        h_scr[...] = (x_ref[...] * (1.0 + sc_ref[...]) + sh_ref[...]).astype(BF16)

    z = _gelu_tanh(jnp.dot(h_scr[...], w_ref[...], preferred_element_type=F32) + b_ref[...])

    @pl.when(j == 0)
    def _():
        u_ref[...] = z.astype(BF16)

    @pl.when(j == 1)
    def _():
        v_ref[...] = _layer_norm_rows(z, lng_ref[...], lnb_ref[...]).astype(BF16)


def sg_in(x, mod4, w, b, ln_g, ln_b, seq, tm=512):
    t, d = x.shape
    half = w.shape[1] // 2
    rows_per_b = seq // tm
    modspec = lambda kk: pl.BlockSpec((None, None, 1, d), lambda i, j: (i // rows_per_b, kk, 0, 0))
    return pl.pallas_call(
        _sg_in_kernel,
        grid=(t // tm, 2),
        in_specs=[
            pl.BlockSpec((tm, d), lambda i, j: (i, 0)),
            modspec(1), modspec(0),
            pl.BlockSpec((d, half), lambda i, j: (0, j)),
            pl.BlockSpec((1, half), lambda i, j: (0, j)),
            _const_spec((1, half)), _const_spec((1, half)),
        ],
        out_specs=[pl.BlockSpec((tm, half), lambda i, j: (i, 0)),
                   pl.BlockSpec((tm, half), lambda i, j: (i, 0))],
        out_shape=[jax.ShapeDtypeStruct((t, half), BF16), jax.ShapeDtypeStruct((t, half), BF16)],
        scratch_shapes=[pltpu.VMEM((tm, d), BF16)],
        compiler_params=_cparams(("parallel", "arbitrary")),
        name="sg_in",
    )(x, mod4, mod4, w, b.reshape(1, 2 * half), ln_g.reshape(1, half), ln_b.reshape(1, half))


def _sg_mix_kernel(u_ref, v_ref, w_ref, bt_ref, o_ref):
    nblk = u_ref.shape[0] // SG_BLOCK
    gw = u_ref.shape[1] // SG_GROUPS
    ii = lax.broadcasted_iota(jnp.int32, (SG_BLOCK, SG_BLOCK), 0)
    jj = lax.broadcasted_iota(jnp.int32, (SG_BLOCK, SG_BLOCK), 1)
    allowed = (jj // CHUNK) <= (ii // CHUNK)
    for g in range(SG_GROUPS):
        w = jnp.where(allowed, w_ref[g], 0.0).astype(BF16)
        bias = bt_ref[:, g:g + 1]
        for n in range(nblk):
            rows = slice(n * SG_BLOCK, (n + 1) * SG_BLOCK)
            cols = slice(g * gw, (g + 1) * gw)
            mixed = jnp.dot(w, v_ref[rows, cols], preferred_element_type=F32) + bias
            o_ref[rows, cols] = (u_ref[rows, cols].astype(F32) * mixed).astype(BF16)


def sg_mix(u, v, w_sp, b_sp, tm=256):
    t, half = u.shape
    return pl.pallas_call(
        _sg_mix_kernel,
        grid=(t // tm,),
        in_specs=[
            pl.BlockSpec((tm, half), lambda i: (i, 0)),
            pl.BlockSpec((tm, half), lambda i: (i, 0)),
            _const_spec(w_sp.shape),
            _const_spec((SG_BLOCK, SG_GROUPS)),
        ],
        out_specs=pl.BlockSpec((tm, half), lambda i: (i, 0)),
        out_shape=jax.ShapeDtypeStruct((t, half), BF16),
        compiler_params=_cparams(("parallel",)),
        name="sg_mix",
    )(u, v, w_sp, b_sp.T)


def _ml_qkv_kernel(xm_ref, cw_ref, cb_ref, wq_ref, wk_ref, wv_ref, wg_ref, bg_ref,
                   q_ref, k_ref, v_ref, xc_ref, g_ref, buf_ref):
    first = pl.program_id(1) == 0
    width = xm_ref.shape[1]
    hd = width // ML_HEADS
    xm = xm_ref[...]
    conv = _causal_conv(buf_ref, xm.astype(F32), cw_ref, cb_ref, first)
    xc = (conv * _sigmoid(conv)).astype(BF16)
    xc_ref[...] = xc
    for h in range(ML_HEADS):
        cols = slice(h * hd, (h + 1) * hd)
        q_ref[:, cols] = jnp.dot(xc[:, cols], wq_ref[h], preferred_element_type=F32).astype(BF16)
        k_ref[:, cols] = jnp.dot(xc[:, cols], wk_ref[h], preferred_element_type=F32).astype(BF16)
        v_ref[:, cols] = jnp.dot(xm[:, cols], wv_ref[h], preferred_element_type=F32).astype(BF16)
    gates = (jnp.dot(q_ref[...], wg_ref[0], preferred_element_type=F32)
             + jnp.dot(k_ref[...], wg_ref[1], preferred_element_type=F32)
             + jnp.dot(v_ref[...], wg_ref[2], preferred_element_type=F32) + bg_ref[...])
    is_f = lax.broadcasted_iota(jnp.int32, gates.shape, 1) >= ML_HEADS
    g_ref[...] = jnp.where(is_f, -_softplus(-gates), gates)


def _ml_headwise_weights(w):
    nqb, bs, _ = w.shape
    per = nqb // ML_HEADS
    w4 = w.reshape(ML_HEADS, per, bs, bs)
    eye = jnp.eye(per, dtype=w.dtype)
    return jnp.einsum("hpcd,pq->hpcqd", w4, eye).reshape(ML_HEADS, per * bs, per * bs).astype(BF16)


def ml_qkv(zz, conv_w, conv_b, wq, wk, wv, w_gates, b_gates, bsz, seq, ts=256):
    t = zz.shape[0]
    width = conv_w.shape[1]
    hd = width // ML_HEADS
    spb = seq // ts
    row = lambda b, s: (b * spb + s, 0)
    act = lambda: pl.BlockSpec((ts, width), row)
    return pl.pallas_call(
        _ml_qkv_kernel,
        grid=(bsz, spb),
        in_specs=[
            act(),
            _const_spec((CONV_W, width)), _const_spec((1, width)),
            _const_spec((ML_HEADS, hd, hd)), _const_spec((ML_HEADS, hd, hd)), _const_spec((ML_HEADS, hd, hd)),
            _const_spec((3, width, 2 * ML_HEADS)), _const_spec((1, 2 * ML_HEADS)),
        ],
        out_specs=[act(), act(), act(), act(), pl.BlockSpec((ts, 2 * ML_HEADS), row)],
        out_shape=[jax.ShapeDtypeStruct((t, width), BF16)] * 4
                  + [jax.ShapeDtypeStruct((t, 2 * ML_HEADS), F32)],
        scratch_shapes=[pltpu.VMEM((ts + SUBLANES, width), F32)],
        compiler_params=_cparams(("parallel", "arbitrary")),
        name="ml_qkv",
    )(zz, conv_w, conv_b.reshape(1, width), wq, wk, wv,
      w_gates.reshape(3, width, 2 * ML_HEADS).astype(BF16), b_gates.reshape(1, 2 * ML_HEADS))


def _ml_chunk_kernel(q_ref, k_ref, v_ref, xc_ref, z_ref, gc_ref, gr_ref, ng_ref, skip_ref,
                     o_ref, c_scr, n_scr, m_scr):
    L, width = q_ref.shape
    hd = width // ML_HEADS
    scale = hd ** -0.5

    @pl.when(pl.program_id(1) == 0)
    def _():
        c_scr[...] = jnp.zeros_like(c_scr)
        n_scr[...] = jnp.zeros_like(n_scr)
        m_scr[...] = jnp.zeros_like(m_scr)

    ri = lax.broadcasted_iota(jnp.int32, (L, L), 0)
    ci = lax.broadcasted_iota(jnp.int32, (L, L), 1)
    causal = ri >= ci
    lower = causal.astype(F32)
    upper = (ri <= ci).astype(F32)
    gc = gc_ref[...]
    gr = gr_ref[...]
    bcol = jnp.dot(lower, gc[:, ML_HEADS:], preferred_element_type=F32, precision=lax.Precision.HIGHEST)
    brow = jnp.dot(gr[ML_HEADS:, :], upper, preferred_element_type=F32, precision=lax.Precision.HIGHEST)

    for h in range(ML_HEADS):
        cols = slice(h * hd, (h + 1) * hd)
        bc = bcol[:, h:h + 1]
        br = brow[h:h + 1, :]
        lir = gr[h:h + 1, :]
        lic = gc[:, h:h + 1]
        m_prev = m_scr[h:h + 1, 0:1]
        g = bc + m_prev
        dmat = jnp.where(causal, bc - br + lir, NEG)
        m_t = jnp.maximum(g, jnp.max(dmat, axis=-1, keepdims=True))
        w_intra = jnp.exp(dmat - m_t)
        w_inter = jnp.exp(g - m_t)
        qh = q_ref[:, cols]
        kh = k_ref[:, cols]
        vh = v_ref[:, cols]
        s = lax.dot_general(qh, kh, (((1,), (1,)), ((), ())), preferred_element_type=F32) * scale * w_intra
        c_old = c_scr[h]
        n_old = n_scr[h:h + 1, :]
        num = (jnp.dot(s.astype(BF16), vh, preferred_element_type=F32)
               + w_inter * jnp.dot(qh, c_old.astype(BF16), preferred_element_type=F32))
        den = (jnp.sum(s, axis=-1, keepdims=True)
               + w_inter * jnp.sum(qh.astype(F32) * n_old, axis=-1, keepdims=True))
        out = num / jnp.maximum(jnp.abs(den), jnp.exp(-m_t))

        b_last = bc[L - 1:L, :]
        dec = b_last - bc + lic
        m_new = jnp.maximum(b_last + m_prev, jnp.max(dec, axis=0, keepdims=True))
        ws = jnp.exp(dec - m_new)
        wc = jnp.exp(b_last + m_prev - m_new)
        kw = kh.astype(F32) * (ws * scale)
        c_scr[h] = wc * c_old + lax.dot_general(kw.astype(BF16), vh, (((0,), (0,)), ((), ())),
                                                preferred_element_type=F32)
        n_scr[h:h + 1, :] = wc * n_old + jnp.sum(kw, axis=0, keepdims=True)
        m_scr[h:h + 1, :] = jnp.broadcast_to(m_new, (1, m_scr.shape[1]))

        mu = jnp.mean(out, axis=-1, keepdims=True)
        dlt = out - mu
        var = jnp.mean(dlt * dlt, axis=-1, keepdims=True)
        hn = dlt * lax.rsqrt(var + LN_EPS) * ng_ref[:, cols]
        hn = hn + skip_ref[:, cols] * xc_ref[:, cols].astype(F32)
        o_ref[:, cols] = (_sigmoid(z_ref[:, cols].astype(F32)) * hn).astype(BF16)


def ml_chunk(q, k, v, xc, zz, gates, gates_t, norm_g, skip, bsz, seq):
    t, width = q.shape
    hd = width // ML_HEADS
    L = ML_L
    cpb = seq // L
    row = lambda b, c: (b * cpb + c, 0)
    act = lambda: pl.BlockSpec((L, width), row)
    return pl.pallas_call(
        _ml_chunk_kernel,
        grid=(bsz, cpb),
        in_specs=[
            act(), act(), act(), act(),
            pl.BlockSpec((L, width), lambda b, c: (b * cpb + c, 1)),
            pl.BlockSpec((L, 2 * ML_HEADS), row),
            pl.BlockSpec((2 * ML_HEADS, L), lambda b, c: (0, b * cpb + c)),
            _const_spec((1, width)), _const_spec((1, width)),
        ],
        out_specs=act(),
        out_shape=jax.ShapeDtypeStruct((t, width), BF16),
        scratch_shapes=[pltpu.VMEM((ML_HEADS, hd, hd), F32), pltpu.VMEM((ML_HEADS, hd), F32),
                        pltpu.VMEM((ML_HEADS, 128), F32)],
        compiler_params=_cparams(("parallel", "arbitrary")),
        name="ml_chunk",
    )(q, k, v, xc, zz, gates, gates_t, norm_g.reshape(1, width), skip.reshape(1, width))


def _router_kernel(h_ref, w_ref, b_ref, eid_ref, prob_ref, rank_ref, cnt_ref, cnt_scr):
    tr = h_ref.shape[0]
    ne = w_ref.shape[0]

    @pl.when(pl.program_id(0) == 0)
    def _():
        cnt_scr[...] = jnp.zeros_like(cnt_scr)

    logits = lax.dot_general(w_ref[...], h_ref[...], (((1,), (1,)), ((), ())),
                             preferred_element_type=F32) + b_ref[...]
    sub = lax.broadcasted_iota(jnp.int32, (ne, tr), 0)
    work = logits
    sel = jnp.zeros((ne, tr), F32)
    hits = []
    top = None
    for k in range(TOP_K):
        m = jnp.max(work, axis=0, keepdims=True)
        idx = jnp.min(jnp.where(work == m, sub, ne), axis=0, keepdims=True)
        hit = sub == idx
        hits.append((hit, idx))
        sel = jnp.where(hit, 1.0, sel)
        work = jnp.where(hit, -jnp.inf, work)
        if k == 0:
            top = m
    e = sel * jnp.exp(logits - top)
    p = e / jnp.sum(e, axis=0, keepdims=True)

    ri = lax.broadcasted_iota(jnp.int32, (tr, tr), 0)
    ci = lax.broadcasted_iota(jnp.int32, (tr, tr), 1)
    strict_upper = (ri < ci).astype(BF16)
    before = jnp.dot(sel.astype(BF16), strict_upper, preferred_element_type=F32)
    rank = cnt_scr[...] + before
    cnt_scr[...] = cnt_scr[...] + jnp.sum(sel, axis=1, keepdims=True)
    cnt_ref[...] = cnt_scr[...].astype(jnp.int32)

    row = lax.broadcasted_iota(jnp.int32, (SUBLANES, tr), 0)
    eid = jnp.zeros((SUBLANES, tr), jnp.int32)
    prob = jnp.zeros((SUBLANES, tr), F32)
    rnk = jnp.zeros((SUBLANES, tr), F32)
    for k, (hit, idx) in enumerate(hits):
        slot = row == k
        eid = jnp.where(slot, idx, eid)
        prob = jnp.where(slot, jnp.sum(jnp.where(hit, p, 0.0), axis=0, keepdims=True), prob)
        rnk = jnp.where(slot, jnp.sum(jnp.where(hit, rank, 0.0), axis=0, keepdims=True), rnk)
    eid_ref[...] = eid
    prob_ref[...] = prob
    rank_ref[...] = rnk.astype(jnp.int32)


def router(h2, w_router, b_router, tr=512):
    t, d = h2.shape
    ne = w_router.shape[1]
    tr = min(tr, t)
    col = lambda i: (0, i)
    return pl.pallas_call(
        _router_kernel,
        grid=(t // tr,),
        in_specs=[pl.BlockSpec((tr, d), lambda i: (i, 0)), _const_spec((ne, d)), _const_spec((ne, 1))],
        out_specs=[pl.BlockSpec((SUBLANES, tr), col), pl.BlockSpec((SUBLANES, tr), col),
                   pl.BlockSpec((SUBLANES, tr), col), pl.BlockSpec((ne, 1), lambda i: (0, 0))],
        out_shape=[jax.ShapeDtypeStruct((SUBLANES, t), jnp.int32), jax.ShapeDtypeStruct((SUBLANES, t), F32),
                   jax.ShapeDtypeStruct((SUBLANES, t), jnp.int32), jax.ShapeDtypeStruct((ne, 1), jnp.int32)],
        scratch_shapes=[pltpu.VMEM((ne, 1), F32)],
        compiler_params=_cparams(("arbitrary",)),
        name="router",
    )(h2, w_router.T.astype(BF16), b_router.reshape(ne, 1))


def _expert_kernel(te_ref, nu_ref, x_ref, wgu_ref, bgu_ref, wd_ref, bd_ref, o_ref, wgu_bf, wd_bf):
    i = pl.program_id(0)
    used = i < nu_ref[0]
    new_expert = jnp.logical_or(i == 0, te_ref[i] != te_ref[jnp.maximum(i - 1, 0)])

    @pl.when(jnp.logical_and(used, new_expert))
    def _():
        wgu_bf[...] = wgu_ref[...].astype(BF16)
        wd_bf[...] = wd_ref[...].astype(BF16)

    @pl.when(used)
    def _():
        f = wd_ref.shape[0]
        gu = jnp.dot(x_ref[...], wgu_bf[...], preferred_element_type=F32) + bgu_ref[...]
        gate = jnp.minimum(gu[:, :f], SWIGLU_LIMIT)
        up = jnp.clip(gu[:, f:], -SWIGLU_LIMIT, SWIGLU_LIMIT)
        act = gate * _sigmoid(SWIGLU_ALPHA * gate) * (up + 1.0)
        y = jnp.dot(act.astype(BF16), wd_bf[...], preferred_element_type=F32) + bd_ref[...]
        o_ref[...] = y.astype(BF16)

    @pl.when(jnp.logical_not(used))
    def _():
        o_ref[...] = jnp.zeros_like(o_ref)


def experts(xs, tile_e, n_used, w_gu, b_gu, w_d, b_d, layer):
    p, d = xs.shape
    depth, ne, _, f2 = w_gu.shape
    f = f2 // 2
    te = EXPERT_TILE
    grid_spec = pltpu.PrefetchScalarGridSpec(
        num_scalar_prefetch=2,
        grid=(p // te,),
        in_specs=[
            pl.BlockSpec((te, d), lambda i, e, n: (i, 0)),
            pl.BlockSpec((None, None, d, f2), lambda i, e, n: (layer, e[i], 0, 0)),
            pl.BlockSpec((None, None, 1, f2), lambda i, e, n: (layer, e[i], 0, 0)),
            pl.BlockSpec((None, None, f, d), lambda i, e, n: (layer, e[i], 0, 0)),
            pl.BlockSpec((None, None, 1, d), lambda i, e, n: (layer, e[i], 0, 0)),
        ],
        out_specs=pl.BlockSpec((te, d), lambda i, e, n: (i, 0)),
        scratch_shapes=[pltpu.VMEM((d, f2), BF16), pltpu.VMEM((f, d), BF16)],
    )
    return pl.pallas_call(
        _expert_kernel,
        grid_spec=grid_spec,
        out_shape=jax.ShapeDtypeStruct((p, d), BF16),
        compiler_params=_cparams(("arbitrary",)),
        name="experts",
    )(tile_e, n_used, xs, w_gu, b_gu.reshape(depth, ne, 1, f2), w_d, b_d.reshape(depth, ne, 1, d))


def _combine_ln_kernel(y0_ref, y1_ref, y2_ref, y3_ref, p_ref, x_ref, g_ref, lng_ref, lnb_ref, xo_ref,
                       *, alpha):
    p = p_ref[...]
    y = (p[:, 0:1] * y0_ref[...].astype(F32) + p[:, 1:2] * y1_ref[...].astype(F32)
         + p[:, 2:3] * y2_ref[...].astype(F32) + p[:, 3:4] * y3_ref[...].astype(F32))
    v = alpha * x_ref[...] + (1.0 + g_ref[...]) * y
    xo_ref[...] = _layer_norm_rows(v, lng_ref[...], lnb_ref[...])


def combine_ln(ys, prob, x, mod4, ln_g, ln_b, seq, alpha, tm=256):
    t, d = x.shape
    ne = prob.shape[1]
    rows_per_b = seq // tm
    row = lambda i: (i, 0)
    return pl.pallas_call(
        functools.partial(_combine_ln_kernel, alpha=alpha),
        grid=(t // tm,),
        in_specs=[pl.BlockSpec((tm, d), row)] * 4 + [
            pl.BlockSpec((tm, ne), row),
            pl.BlockSpec((tm, d), row),
            pl.BlockSpec((None, None, 1, d), lambda i: (i // rows_per_b, 5, 0, 0)),
            _const_spec((1, d)), _const_spec((1, d)),
        ],
        out_specs=pl.BlockSpec((tm, d), row),
        out_shape=jax.ShapeDtypeStruct((t, d), F32),
        compiler_params=_cparams(("parallel",)),
        name="combine_ln",
    )(*ys, prob, x, mod4, ln_g.reshape(1, d), ln_b.reshape(1, d))


def moe_layer(h2, x, mod4, w_router, b_router, w_gu, b_gu, w_d, b_d, layer, ln_g, ln_b, seq, alpha):
    t, d = h2.shape
    ne = w_router.shape[1]
    te = EXPERT_TILE
    eid, prob, rank, counts = router(h2, w_router, b_router)
    counts = counts[:, 0]
    padded = ((counts + te - 1) // te) * te
    ends = jnp.cumsum(padded)
    starts = ends - padded
    prob = prob.T
    dest4 = starts.at[eid[:TOP_K]].get(mode=_IN_BOUNDS) + rank[:TOP_K]
    n_slots = t * TOP_K + ne * te
    tok = jnp.broadcast_to(jnp.arange(t, dtype=jnp.int32)[None, :], (TOP_K, t))
    src = jnp.zeros((n_slots,), jnp.int32).at[dest4.reshape(-1)].set(
        tok.reshape(-1), unique_indices=True, mode=_IN_BOUNDS)
    n_tiles = n_slots // te
    tile_start = jnp.arange(n_tiles, dtype=jnp.int32) * te
    tile_e = jnp.sum((ends[None, :] <= tile_start[:, None]).astype(jnp.int32), axis=1)
    n_used = (ends[-1] // te).astype(jnp.int32).reshape(1)
    last_e = jnp.sum((ends <= ends[-1] - 1).astype(jnp.int32))
    tile_e = jnp.where(tile_start < ends[-1], tile_e, last_e).astype(jnp.int32)

    xs = h2.at[src].get(mode=_IN_BOUNDS)
    ysorted = experts(xs, tile_e, n_used, w_gu, b_gu, w_d, b_d, layer)
    ys = [ysorted.at[dest4[k]].get(mode=_IN_BOUNDS) for k in range(TOP_K)]
    return combine_ln(ys, prob, x, mod4, ln_g, ln_b, seq, alpha)


def kernel(x, c, ada_w, ada_b, ln1_g, ln1_b, ln2_g, ln2_b, rg_w_in, rg_conv_w, rg_conv_b, rg_w_rgate, rg_b_rgate, rg_w_igate, rg_b_igate, rg_lam, rg_w_out, sg_w_in, sg_b_in, sg_ln_g, sg_ln_b, sg_w_sp, sg_b_sp, sg_w_out, sg_b_out, ml_w_in, ml_conv_w, ml_conv_b, ml_w_q, ml_w_k, ml_w_v, ml_w_gates, ml_b_gates, ml_skip, ml_norm_g, ml_w_out, moe_w_router, moe_b_router, moe_w_gate_up, moe_b_gate_up, moe_w_down, moe_b_down):
    bsz, seq, d = x.shape
    depth = ada_w.shape[0]
    alpha = (2 * depth) ** 0.25
    t = bsz * seq
    xt = x.reshape(t, d)
    mod = ada_mod(c, ada_w, ada_b)
    zero_bias = jnp.zeros((d,), F32)

    for i in range(depth):
        mod4 = mod[i].reshape(bsz, 6, 1, d)
        kind, j = i % 3, i // 3
        if kind == 0:
            z = mm_in(xt, mod4, 1, 0, rg_w_in[j].astype(BF16), seq)
            w_ri = _rg_gate_weights(rg_w_rgate[j], rg_w_igate[j])
            a = rg_core(z, rg_conv_w[j], rg_conv_b[j], w_ri, rg_b_rgate[j], rg_b_igate[j], rg_lam[j], bsz, seq)
            w_out, b_out = rg_w_out[j], zero_bias
        elif kind == 1:
            u, v = sg_in(xt, mod4, sg_w_in[j].astype(BF16), sg_b_in[j], sg_ln_g[j], sg_ln_b[j], seq)
            a = sg_mix(u, v, sg_w_sp[j], sg_b_sp[j])
            w_out, b_out = sg_w_out[j], sg_b_out[j]
        else:
            zz = mm_in(xt, mod4, 1, 0, ml_w_in[j].astype(BF16), seq)
            q, k, v, xc, gates = ml_qkv(zz, ml_conv_w[j], ml_conv_b[j], _ml_headwise_weights(ml_w_q[j]),
                                        _ml_headwise_weights(ml_w_k[j]), _ml_headwise_weights(ml_w_v[j]),
                                        ml_w_gates[j], ml_b_gates[j], bsz, seq)
            a = ml_chunk(q, k, v, xc, zz, gates, gates.T, ml_norm_g[j], ml_skip[j], bsz, seq)
            w_out, b_out = ml_w_out[j], zero_bias
        xt, h2 = mm_res_ln(a, w_out.astype(BF16), b_out, xt, mod4, 2, 4, 3, ln1_g[i], ln1_b[i], seq, alpha)
        xt = moe_layer(h2, xt, mod4, moe_w_router[i], moe_b_router[i], moe_w_gate_up,
                       moe_b_gate_up, moe_w_down, moe_b_down, i, ln2_g[i], ln2_b[i], seq, alpha)
    return xt.reshape(bsz, seq, d)
```

```python
import functools
import math

import jax
import jax.numpy as jnp
from jax import lax
from jax.experimental import pallas as pl
from jax.experimental.pallas import tpu as pltpu

F32 = jnp.float32
BF16 = jnp.bfloat16

LN_EPS = 1e-5
RG_C = 8.0
CONV_W = 4
CHUNK = 64
SG_BLOCK = 128
SG_GROUPS = 8
ML_HEADS = 8
TOP_K = 4
SWIGLU_LIMIT = 7.0
SWIGLU_ALPHA = 1.702

SUBLANES = 8
RG_SUPER = 640
ML_L = 256
EXPERT_TILE = 512
BATCH_GROUPS = 2
VMEM_LIMIT = 56 * 1024 * 1024
NEG = -1e30
_IN_BOUNDS = "promise_in_bounds"


def _cparams(sem):
    return pltpu.CompilerParams(dimension_semantics=sem, vmem_limit_bytes=VMEM_LIMIT)


def _sigmoid(x):
    return 1.0 / (1.0 + jnp.exp(-x))


def _gelu_tanh(x):
    return 0.5 * x * (1.0 + jnp.tanh(math.sqrt(2.0 / math.pi) * (x + 0.044715 * (x * x * x))))


def _softplus(x):
    return jnp.maximum(x, 0.0) + jnp.log1p(jnp.exp(-jnp.abs(x)))


def _layer_norm_rows(v, g, b):
    mu = jnp.mean(v, axis=-1, keepdims=True)
    d = v - mu
    var = jnp.mean(d * d, axis=-1, keepdims=True)
    return d * lax.rsqrt(var + LN_EPS) * g + b


def _const_spec(shape):
    nd = len(shape)
    return pl.BlockSpec(shape, lambda *_: (0,) * nd, pipeline_mode=pl.Buffered(1))


def _after(kernel_fn, n_in, deps):
    nd = len(deps)
    if nd == 0:
        return kernel_fn

    def body(*refs):
        return kernel_fn(*refs[:n_in], *refs[n_in + nd:])

    return body


def _dep_specs(deps):
    return [pl.BlockSpec(memory_space=pl.ANY)] * len(deps)


def _ada_kernel(c_ref, w_ref, b_ref, o_ref):
    c = c_ref[...]
    cond = c * _sigmoid(c)
    o_ref[...] = jnp.dot(cond.astype(BF16), w_ref[...].astype(BF16),
                         preferred_element_type=F32) + b_ref[...]


def ada_mod(c, ada_w, ada_b, tn=1024):
    depth, d, n = ada_w.shape
    bsz = c.shape[0]
    return pl.pallas_call(
        _ada_kernel,
        grid=(depth, n // tn),
        in_specs=[
            pl.BlockSpec((bsz, d), lambda l, j: (0, 0)),
            pl.BlockSpec((None, d, tn), lambda l, j: (l, 0, j)),
            pl.BlockSpec((None, 1, tn), lambda l, j: (l, 0, j)),
        ],
        out_specs=pl.BlockSpec((None, bsz, tn), lambda l, j: (l, 0, j)),
        out_shape=jax.ShapeDtypeStruct((depth, bsz, n), F32),
        compiler_params=_cparams(("parallel", "parallel")),
        name="ada_mod",
    )(c, ada_w, ada_b.reshape(depth, 1, n))


def _mm_in_kernel(x_ref, sc_ref, sh_ref, w_ref, o_ref, h_scr):
    @pl.when(pl.program_id(1) == 0)
    def _():
        h_scr[...] = (x_ref[...] * (1.0 + sc_ref[...]) + sh_ref[...]).astype(BF16)

    o_ref[...] = jnp.dot(h_scr[...], w_ref[...], preferred_element_type=F32).astype(o_ref.dtype)


def mm_in(x, mod4, k_sc, k_sh, w, seq, row_off=0, deps=(), tm=1024, tn=512):
    t, d = mod4.shape[0] * seq, x.shape[1]
    n = w.shape[1]
    tm = min(tm, seq)
    rows_per_b = seq // tm
    blk_off = row_off // tm
    return pl.pallas_call(
        _after(_mm_in_kernel, 4, deps),
        grid=(t // tm, n // tn),
        in_specs=[
            pl.BlockSpec((tm, d), lambda i, j: (i + blk_off, 0)),
            pl.BlockSpec((None, None, 1, d), lambda i, j: (i // rows_per_b, k_sc, 0, 0)),
            pl.BlockSpec((None, None, 1, d), lambda i, j: (i // rows_per_b, k_sh, 0, 0)),
            pl.BlockSpec((d, tn), lambda i, j: (0, j)),
        ] + _dep_specs(deps),
        out_specs=pl.BlockSpec((tm, tn), lambda i, j: (i, j)),
        out_shape=jax.ShapeDtypeStruct((t, n), BF16),
        scratch_shapes=[pltpu.VMEM((tm, d), BF16)],
        compiler_params=_cparams(("parallel", "arbitrary")),
        name="mm_in",
    )(x, mod4, mod4, w, *deps)


def _mm_res_ln_kernel(a_ref, w_ref, b_ref, x_ref, g_ref, lng_ref, lnb_ref, sc_ref, sh_ref,
                      xo_ref, ho_ref, *, alpha):
    y = jnp.dot(a_ref[...], w_ref[...], preferred_element_type=F32) + b_ref[...]
    v = alpha * x_ref[...] + (1.0 + g_ref[...]) * y
    xn = _layer_norm_rows(v, lng_ref[...], lnb_ref[...])
    xo_ref[...] = xn
    ho_ref[...] = (xn * (1.0 + sc_ref[...]) + sh_ref[...]).astype(BF16)


def mm_res_ln(a, w, bias, x, mod4, k_gate, k_sc, k_sh, ln_g, ln_b, seq, alpha, row_off=0, tm=256):
    t, k = a.shape
    d = x.shape[1]
    rows_per_b = seq // tm
    blk_off = row_off // tm
    modspec = lambda kk: pl.BlockSpec((None, None, 1, d), lambda i: (i // rows_per_b, kk, 0, 0))
    return pl.pallas_call(
        functools.partial(_mm_res_ln_kernel, alpha=alpha),
        grid=(t // tm,),
        in_specs=[
            pl.BlockSpec((tm, k), lambda i: (i, 0)),
            _const_spec((k, d)),
            _const_spec((1, d)),
            pl.BlockSpec((tm, d), lambda i: (i + blk_off, 0)),
            modspec(k_gate),
            _const_spec((1, d)),
            _const_spec((1, d)),
            modspec(k_sc),
            modspec(k_sh),
        ],
        out_specs=[pl.BlockSpec((tm, d), lambda i: (i, 0)), pl.BlockSpec((tm, d), lambda i: (i, 0))],
        out_shape=[jax.ShapeDtypeStruct((t, d), F32), jax.ShapeDtypeStruct((t, d), BF16)],
        compiler_params=_cparams(("parallel",)),
        name="mm_res_ln",
    )(a, w, bias.reshape(1, d), x, mod4, ln_g.reshape(1, d), ln_b.reshape(1, d), mod4, mod4)


def _causal_conv(buf_ref, x, w_ref, b_ref, first):
    ts = x.shape[0]

    @pl.when(first)
    def _():
        buf_ref[0:SUBLANES, :] = jnp.zeros((SUBLANES, x.shape[1]), F32)

    buf_ref[SUBLANES:SUBLANES + ts, :] = x
    y = b_ref[...] + w_ref[CONV_W - 1:CONV_W, :] * x
    for back in range(1, CONV_W):
        y = y + w_ref[CONV_W - 1 - back:CONV_W - back, :] * buf_ref[SUBLANES - back:SUBLANES - back + ts, :]
    buf_ref[0:SUBLANES, :] = buf_ref[ts:ts + SUBLANES, :]
    return y


def _scan_rows(a, u, h0):
    ts, c = a.shape
    groups = ts // SUBLANES
    a3 = a.reshape(groups, SUBLANES, c)
    u3 = u.reshape(groups, SUBLANES, c)
    rid = lax.broadcasted_iota(jnp.int32, a3.shape, 1)
    for k in (1, 2, 4):
        keep = rid >= k
        a_sh = jnp.where(keep, pltpu.roll(a3, k, 1), 1.0)
        u_sh = jnp.where(keep, pltpu.roll(u3, k, 1), 0.0)
        u3 = u3 + a3 * u_sh
        a3 = a3 * a_sh
    h = h0
    outs = []
    for j in range(groups):
        hj = u3[j] + a3[j] * h
        outs.append(hj)
        h = hj[SUBLANES - 1:SUBLANES, :]
    return jnp.concatenate(outs, axis=0), h


def _rg_core_kernel(gate_ref, rec_ref, cw_ref, cb_ref, wri_ref, br_ref, bi_ref, lam_ref,
                    o_ref, buf_ref, h_ref):
    first = pl.program_id(2) == 0
    c = o_ref.shape[1]
    xr = _causal_conv(buf_ref, rec_ref[...].astype(F32), cw_ref, cb_ref, first)
    pre = jnp.dot(xr.astype(BF16), wri_ref[...], preferred_element_type=F32)
    r = _sigmoid(pre[:, :c] + br_ref[...])
    i = _sigmoid(pre[:, c:] + bi_ref[...])
    log_a = (-RG_C) * r * _softplus(-lam_ref[...])
    a = jnp.exp(log_a)
    u = jnp.sqrt(-jnp.tanh(log_a) * (a * a + 1.0)) * (i * xr)

    @pl.when(first)
    def _():
        h_ref[...] = jnp.zeros_like(h_ref)

    hs, h_last = _scan_rows(a, u, h_ref[...])
    h_ref[...] = h_last
    o_ref[...] = (_gelu_tanh(gate_ref[...].astype(F32)) * hs).astype(BF16)


def rg_core(z, conv_w, conv_b, w_ri, b_r, b_i, lam, bsz, seq, ts=256):
    t = z.shape[0]
    width = conv_w.shape[1]
    nsup = width // RG_SUPER
    spb = seq // ts
    row = lambda c, b, s: b * spb + s
    vec = lambda: pl.BlockSpec((1, RG_SUPER), lambda c, b, s: (0, c))
    return pl.pallas_call(
        _rg_core_kernel,
        grid=(nsup, bsz, spb),
        in_specs=[
            pl.BlockSpec((ts, RG_SUPER), lambda c, b, s: (row(c, b, s), c)),
            pl.BlockSpec((ts, RG_SUPER), lambda c, b, s: (row(c, b, s), nsup + c)),
            pl.BlockSpec((CONV_W, RG_SUPER), lambda c, b, s: (0, c)),
            vec(),
            pl.BlockSpec((None, RG_SUPER, 2 * RG_SUPER), lambda c, b, s: (c, 0, 0)),
            vec(), vec(), vec(),
        ],
        out_specs=pl.BlockSpec((ts, RG_SUPER), lambda c, b, s: (row(c, b, s), c)),
        out_shape=jax.ShapeDtypeStruct((t, width), BF16),
        scratch_shapes=[pltpu.VMEM((ts + SUBLANES, RG_SUPER), F32), pltpu.VMEM((1, RG_SUPER), F32)],
        compiler_params=_cparams(("parallel", "parallel", "arbitrary")),
        name="rg_core",
    )(z, z, conv_w, conv_b.reshape(1, width), w_ri, b_r.reshape(1, width), b_i.reshape(1, width),
      lam.reshape(1, width))


def _rg_gate_weights(w_r, w_i):
    nb, bs, _ = w_r.shape
    per = RG_SUPER // bs
    nsup = nb // per

    def dense(w):
        w4 = w.reshape(nsup, per, bs, bs)
        eye = jnp.eye(per, dtype=w.dtype)
        return jnp.einsum("spcd,pq->spcqd", w4, eye).reshape(nsup, RG_SUPER, RG_SUPER)

    return jnp.concatenate([dense(w_r), dense(w_i)], axis=-1).astype(BF16)


def _sg_in_kernel(x_ref, sc_ref, sh_ref, w_ref, b_ref, lng_ref, lnb_ref, u_ref, v_ref, h_scr):
    j = pl.program_id(1)

    @pl.when(j == 0)
    def _():
        h_scr[...] = (x_ref[...] * (1.0 + sc_ref[...]) + sh_ref[...]).astype(BF16)

    z = _gelu_tanh(jnp.dot(h_scr[...], w_ref[...], preferred_element_type=F32) + b_ref[...])

    @pl.when(j == 0)
    def _():
        u_ref[...] = z.astype(BF16)

    @pl.when(j == 1)
    def _():
        v_ref[...] = _layer_norm_rows(z, lng_ref[...], lnb_ref[...]).astype(BF16)


def sg_in(x, mod4, w, b, ln_g, ln_b, seq, deps=(), tm=512):
    t, d = x.shape
    half = w.shape[1] // 2
    rows_per_b = seq // tm
    modspec = lambda kk: pl.BlockSpec((None, None, 1, d), lambda i, j: (i // rows_per_b, kk, 0, 0))
    return pl.pallas_call(
        _after(_sg_in_kernel, 7, deps),
        grid=(t // tm, 2),
        in_specs=[
            pl.BlockSpec((tm, d), lambda i, j: (i, 0)),
            modspec(1), modspec(0),
            pl.BlockSpec((d, half), lambda i, j: (0, j)),
            pl.BlockSpec((1, half), lambda i, j: (0, j)),
            _const_spec((1, half)), _const_spec((1, half)),
        ] + _dep_specs(deps),
        out_specs=[pl.BlockSpec((tm, half), lambda i, j: (i, 0)),
                   pl.BlockSpec((tm, half), lambda i, j: (i, 0))],
        out_shape=[jax.ShapeDtypeStruct((t, half), BF16), jax.ShapeDtypeStruct((t, half), BF16)],
        scratch_shapes=[pltpu.VMEM((tm, d), BF16)],
        compiler_params=_cparams(("parallel", "arbitrary")),
        name="sg_in",
    )(x, mod4, mod4, w, b.reshape(1, 2 * half), ln_g.reshape(1, half), ln_b.reshape(1, half), *deps)


def _sg_mix_kernel(u_ref, v_ref, w_ref, bt_ref, o_ref):
    nblk = u_ref.shape[0] // SG_BLOCK
    gw = u_ref.shape[1] // SG_GROUPS
    ii = lax.broadcasted_iota(jnp.int32, (SG_BLOCK, SG_BLOCK), 0)
    jj = lax.broadcasted_iota(jnp.int32, (SG_BLOCK, SG_BLOCK), 1)
    allowed = (jj // CHUNK) <= (ii // CHUNK)
    for g in range(SG_GROUPS):
        w = jnp.where(allowed, w_ref[g], 0.0).astype(BF16)
        bias = bt_ref[:, g:g + 1]
        for n in range(nblk):
            rows = slice(n * SG_BLOCK, (n + 1) * SG_BLOCK)
            cols = slice(g * gw, (g + 1) * gw)
            mixed = jnp.dot(w, v_ref[rows, cols], preferred_element_type=F32) + bias
            o_ref[rows, cols] = (u_ref[rows, cols].astype(F32) * mixed).astype(BF16)


def sg_mix(u, v, w_sp, b_sp, tm=256):
    t, half = u.shape
    return pl.pallas_call(
        _sg_mix_kernel,
        grid=(t // tm,),
        in_specs=[
            pl.BlockSpec((tm, half), lambda i: (i, 0)),
            pl.BlockSpec((tm, half), lambda i: (i, 0)),
            _const_spec(w_sp.shape),
            _const_spec((SG_BLOCK, SG_GROUPS)),
        ],
        out_specs=pl.BlockSpec((tm, half), lambda i: (i, 0)),
        out_shape=jax.ShapeDtypeStruct((t, half), BF16),
        compiler_params=_cparams(("parallel",)),
        name="sg_mix",
    )(u, v, w_sp, b_sp.T)


def _ml_qkv_kernel(xm_ref, cw_ref, cb_ref, wq_ref, wk_ref, wv_ref, wg_ref, bg_ref,
                   q_ref, k_ref, v_ref, xc_ref, g_ref, buf_ref):
    first = pl.program_id(1) == 0
    width = xm_ref.shape[1]
    hd = width // ML_HEADS
    xm = xm_ref[...]
    conv = _causal_conv(buf_ref, xm.astype(F32), cw_ref, cb_ref, first)
    xc = (conv * _sigmoid(conv)).astype(BF16)
    xc_ref[...] = xc
    for h in range(ML_HEADS):
        cols = slice(h * hd, (h + 1) * hd)
        q_ref[:, cols] = jnp.dot(xc[:, cols], wq_ref[h], preferred_element_type=F32).astype(BF16)
        k_ref[:, cols] = jnp.dot(xc[:, cols], wk_ref[h], preferred_element_type=F32).astype(BF16)
        v_ref[:, cols] = jnp.dot(xm[:, cols], wv_ref[h], preferred_element_type=F32).astype(BF16)
    gates = (jnp.dot(q_ref[...], wg_ref[0], preferred_element_type=F32)
             + jnp.dot(k_ref[...], wg_ref[1], preferred_element_type=F32)
             + jnp.dot(v_ref[...], wg_ref[2], preferred_element_type=F32) + bg_ref[...])
    is_f = lax.broadcasted_iota(jnp.int32, gates.shape, 1) >= ML_HEADS
    g_ref[...] = jnp.where(is_f, -_softplus(-gates), gates)


def _ml_headwise_weights(w):
    nqb, bs, _ = w.shape
    per = nqb // ML_HEADS
    w4 = w.reshape(ML_HEADS, per, bs, bs)
    eye = jnp.eye(per, dtype=w.dtype)
    return jnp.einsum("hpcd,pq->hpcqd", w4, eye).reshape(ML_HEADS, per * bs, per * bs).astype(BF16)


def ml_qkv(zz, conv_w, conv_b, wq, wk, wv, w_gates, b_gates, bsz, seq, ts=256):
    t = zz.shape[0]
    width = conv_w.shape[1]
    hd = width // ML_HEADS
    spb = seq // ts
    row = lambda b, s: (b * spb + s, 0)
    act = lambda: pl.BlockSpec((ts, width), row)
    return pl.pallas_call(
        _ml_qkv_kernel,
        grid=(bsz, spb),
        in_specs=[
            act(),
            _const_spec((CONV_W, width)), _const_spec((1, width)),
            _const_spec((ML_HEADS, hd, hd)), _const_spec((ML_HEADS, hd, hd)), _const_spec((ML_HEADS, hd, hd)),
            _const_spec((3, width, 2 * ML_HEADS)), _const_spec((1, 2 * ML_HEADS)),
        ],
        out_specs=[act(), act(), act(), act(), pl.BlockSpec((ts, 2 * ML_HEADS), row)],
        out_shape=[jax.ShapeDtypeStruct((t, width), BF16)] * 4
                  + [jax.ShapeDtypeStruct((t, 2 * ML_HEADS), F32)],
        scratch_shapes=[pltpu.VMEM((ts + SUBLANES, width), F32)],
        compiler_params=_cparams(("parallel", "arbitrary")),
        name="ml_qkv",
    )(zz, conv_w, conv_b.reshape(1, width), wq, wk, wv,
      w_gates.reshape(3, width, 2 * ML_HEADS).astype(BF16), b_gates.reshape(1, 2 * ML_HEADS))


def _ml_chunk_kernel(q_ref, k_ref, v_ref, xc_ref, z_ref, gc_ref, gr_ref, ng_ref, skip_ref,
                     o_ref, c_scr, n_scr, m_scr):
    L, width = q_ref.shape
    hd = width // ML_HEADS
    scale = hd ** -0.5

    @pl.when(pl.program_id(1) == 0)
    def _():
        c_scr[...] = jnp.zeros_like(c_scr)
        n_scr[...] = jnp.zeros_like(n_scr)
        m_scr[...] = jnp.zeros_like(m_scr)

    ri = lax.broadcasted_iota(jnp.int32, (L, L), 0)
    ci = lax.broadcasted_iota(jnp.int32, (L, L), 1)
    causal = ri >= ci
    lower = causal.astype(F32)
    upper = (ri <= ci).astype(F32)
    gc = gc_ref[...]
    gr = gr_ref[...]
    bcol = jnp.dot(lower, gc[:, ML_HEADS:], preferred_element_type=F32, precision=lax.Precision.HIGHEST)
    brow = jnp.dot(gr[ML_HEADS:, :], upper, preferred_element_type=F32, precision=lax.Precision.HIGHEST)

    for h in range(ML_HEADS):
        cols = slice(h * hd, (h + 1) * hd)
        bc = bcol[:, h:h + 1]
        br = brow[h:h + 1, :]
        lir = gr[h:h + 1, :]
        lic = gc[:, h:h + 1]
        m_prev = m_scr[h:h + 1, 0:1]
        g = bc + m_prev
        dmat = jnp.where(causal, bc - br + lir, NEG)
        m_t = jnp.maximum(g, jnp.max(dmat, axis=-1, keepdims=True))
        w_intra = jnp.exp(dmat - m_t)
        w_inter = jnp.exp(g - m_t)
        qh = q_ref[:, cols]
        kh = k_ref[:, cols]
        vh = v_ref[:, cols]
        s = lax.dot_general(qh, kh, (((1,), (1,)), ((), ())), preferred_element_type=F32) * scale * w_intra
        c_old = c_scr[h]
        n_old = n_scr[h:h + 1, :]
        num = (jnp.dot(s.astype(BF16), vh, preferred_element_type=F32)
               + w_inter * jnp.dot(qh, c_old.astype(BF16), preferred_element_type=F32))
        den = (jnp.sum(s, axis=-1, keepdims=True)
               + w_inter * jnp.sum(qh.astype(F32) * n_old, axis=-1, keepdims=True))
        out = num / jnp.maximum(jnp.abs(den), jnp.exp(-m_t))

        b_last = bc[L - 1:L, :]
        dec = b_last - bc + lic
        m_new = jnp.maximum(b_last + m_prev, jnp.max(dec, axis=0, keepdims=True))
        ws = jnp.exp(dec - m_new)
        wc = jnp.exp(b_last + m_prev - m_new)
        kw = kh.astype(F32) * (ws * scale)
        c_scr[h] = wc * c_old + lax.dot_general(kw.astype(BF16), vh, (((0,), (0,)), ((), ())),
                                                preferred_element_type=F32)
        n_scr[h:h + 1, :] = wc * n_old + jnp.sum(kw, axis=0, keepdims=True)
        m_scr[h:h + 1, :] = jnp.broadcast_to(m_new, (1, m_scr.shape[1]))

        mu = jnp.mean(out, axis=-1, keepdims=True)
        dlt = out - mu
        var = jnp.mean(dlt * dlt, axis=-1, keepdims=True)
        hn = dlt * lax.rsqrt(var + LN_EPS) * ng_ref[:, cols]
        hn = hn + skip_ref[:, cols] * xc_ref[:, cols].astype(F32)
        o_ref[:, cols] = (_sigmoid(z_ref[:, cols].astype(F32)) * hn).astype(BF16)


def ml_chunk(q, k, v, xc, zz, gates, gates_t, norm_g, skip, bsz, seq):
    t, width = q.shape
    hd = width // ML_HEADS
    L = ML_L
    cpb = seq // L
    row = lambda b, c: (b * cpb + c, 0)
    act = lambda: pl.BlockSpec((L, width), row)
    return pl.pallas_call(
        _ml_chunk_kernel,
        grid=(bsz, cpb),
        in_specs=[
            act(), act(), act(), act(),
            pl.BlockSpec((L, width), lambda b, c: (b * cpb + c, 1)),
            pl.BlockSpec((L, 2 * ML_HEADS), row),
            pl.BlockSpec((2 * ML_HEADS, L), lambda b, c: (0, b * cpb + c)),
            _const_spec((1, width)), _const_spec((1, width)),
        ],
        out_specs=act(),
        out_shape=jax.ShapeDtypeStruct((t, width), BF16),
        scratch_shapes=[pltpu.VMEM((ML_HEADS, hd, hd), F32), pltpu.VMEM((ML_HEADS, hd), F32),
                        pltpu.VMEM((ML_HEADS, 128), F32)],
        compiler_params=_cparams(("parallel", "arbitrary")),
        name="ml_chunk",
    )(q, k, v, xc, zz, gates, gates_t, norm_g.reshape(1, width), skip.reshape(1, width))


def _router_kernel(h_ref, w_ref, b_ref, eid_ref, prob_ref, rank_ref, cnt_ref, cnt_scr):
    tr = h_ref.shape[0]
    ne = w_ref.shape[0]

    @pl.when(pl.program_id(0) == 0)
    def _():
        cnt_scr[...] = jnp.zeros_like(cnt_scr)

    logits = lax.dot_general(w_ref[...], h_ref[...], (((1,), (1,)), ((), ())),
                             preferred_element_type=F32) + b_ref[...]
    sub = lax.broadcasted_iota(jnp.int32, (ne, tr), 0)
    work = logits
    sel = jnp.zeros((ne, tr), F32)
    hits = []
    top = None
    for k in range(TOP_K):
        m = jnp.max(work, axis=0, keepdims=True)
        idx = jnp.min(jnp.where(work == m, sub, ne), axis=0, keepdims=True)
        hit = sub == idx
        hits.append((hit, idx))
        sel = jnp.where(hit, 1.0, sel)
        work = jnp.where(hit, -jnp.inf, work)
        if k == 0:
            top = m
    e = sel * jnp.exp(logits - top)
    p = e / jnp.sum(e, axis=0, keepdims=True)

    ri = lax.broadcasted_iota(jnp.int32, (tr, tr), 0)
    ci = lax.broadcasted_iota(jnp.int32, (tr, tr), 1)
    strict_upper = (ri < ci).astype(BF16)
    before = jnp.dot(sel.astype(BF16), strict_upper, preferred_element_type=F32)
    rank = cnt_scr[...] + before
    cnt_scr[...] = cnt_scr[...] + jnp.sum(sel, axis=1, keepdims=True)
    cnt_ref[...] = cnt_scr[...].astype(jnp.int32)

    row = lax.broadcasted_iota(jnp.int32, (SUBLANES, tr), 0)
    eid = jnp.zeros((SUBLANES, tr), jnp.int32)
    prob = jnp.zeros((SUBLANES, tr), F32)
    rnk = jnp.zeros((SUBLANES, tr), F32)
    for k, (hit, idx) in enumerate(hits):
        slot = row == k
        eid = jnp.where(slot, idx, eid)
        prob = jnp.where(slot, jnp.sum(jnp.where(hit, p, 0.0), axis=0, keepdims=True), prob)
        rnk = jnp.where(slot, jnp.sum(jnp.where(hit, rank, 0.0), axis=0, keepdims=True), rnk)
    eid_ref[...] = eid
    prob_ref[...] = prob
    rank_ref[...] = rnk.astype(jnp.int32)


def router(h2, w_router, b_router, tr=512):
    t, d = h2.shape
    ne = w_router.shape[1]
    tr = min(tr, t)
    col = lambda i: (0, i)
    return pl.pallas_call(
        _router_kernel,
        grid=(t // tr,),
        in_specs=[pl.BlockSpec((tr, d), lambda i: (i, 0)), _const_spec((ne, d)), _const_spec((ne, 1))],
        out_specs=[pl.BlockSpec((SUBLANES, tr), col), pl.BlockSpec((SUBLANES, tr), col),
                   pl.BlockSpec((SUBLANES, tr), col), pl.BlockSpec((ne, 1), lambda i: (0, 0))],
        out_shape=[jax.ShapeDtypeStruct((SUBLANES, t), jnp.int32), jax.ShapeDtypeStruct((SUBLANES, t), F32),
                   jax.ShapeDtypeStruct((SUBLANES, t), jnp.int32), jax.ShapeDtypeStruct((ne, 1), jnp.int32)],
        scratch_shapes=[pltpu.VMEM((ne, 1), F32)],
        compiler_params=_cparams(("arbitrary",)),
        name="router",
    )(h2, w_router.T.astype(BF16), b_router.reshape(ne, 1))


def _expert_kernel(te_ref, nu_ref, x_ref, wgu_ref, bgu_ref, wd_ref, bd_ref, o_ref, wgu_bf, wd_bf):
    i = pl.program_id(0)
    used = i < nu_ref[0]
    new_expert = jnp.logical_or(i == 0, te_ref[i] != te_ref[jnp.maximum(i - 1, 0)])

    @pl.when(jnp.logical_and(used, new_expert))
    def _():
        wgu_bf[...] = wgu_ref[...].astype(BF16)
        wd_bf[...] = wd_ref[...].astype(BF16)

    @pl.when(used)
    def _():
        f = wd_ref.shape[0]
        gu = jnp.dot(x_ref[...], wgu_bf[...], preferred_element_type=F32) + bgu_ref[...]
        gate = jnp.minimum(gu[:, :f], SWIGLU_LIMIT)
        up = jnp.clip(gu[:, f:], -SWIGLU_LIMIT, SWIGLU_LIMIT)
        act = gate * _sigmoid(SWIGLU_ALPHA * gate) * (up + 1.0)
        y = jnp.dot(act.astype(BF16), wd_bf[...], preferred_element_type=F32) + bd_ref[...]
        o_ref[...] = y.astype(BF16)

    @pl.when(jnp.logical_not(used))
    def _():
        o_ref[...] = jnp.zeros_like(o_ref)


def experts(xs, tile_e, n_used, w_gu, b_gu, w_d, b_d, layer, deps=()):
    p, d = xs.shape
    depth, ne, _, f2 = w_gu.shape
    f = f2 // 2
    te = EXPERT_TILE
    grid_spec = pltpu.PrefetchScalarGridSpec(
        num_scalar_prefetch=2,
        grid=(p // te,),
        in_specs=[
            pl.BlockSpec((te, d), lambda i, e, n: (i, 0)),
            pl.BlockSpec((None, None, d, f2), lambda i, e, n: (layer, e[i], 0, 0)),
            pl.BlockSpec((None, None, 1, f2), lambda i, e, n: (layer, e[i], 0, 0)),
            pl.BlockSpec((None, None, f, d), lambda i, e, n: (layer, e[i], 0, 0)),
            pl.BlockSpec((None, None, 1, d), lambda i, e, n: (layer, e[i], 0, 0)),
        ] + _dep_specs(deps),
        out_specs=pl.BlockSpec((te, d), lambda i, e, n: (i, 0)),
        scratch_shapes=[pltpu.VMEM((d, f2), BF16), pltpu.VMEM((f, d), BF16)],
    )
    return pl.pallas_call(
        _after(_expert_kernel, 7, deps),
        grid_spec=grid_spec,
        out_shape=jax.ShapeDtypeStruct((p, d), BF16),
        compiler_params=_cparams(("arbitrary",)),
        name="experts",
    )(tile_e, n_used, xs, w_gu, b_gu.reshape(depth, ne, 1, f2), w_d, b_d.reshape(depth, ne, 1, d), *deps)


def _combine_ln_kernel(y0_ref, y1_ref, y2_ref, y3_ref, p_ref, x_ref, g_ref, lng_ref, lnb_ref, xo_ref,
                       *, alpha):
    p = p_ref[...]
    y = (p[:, 0:1] * y0_ref[...].astype(F32) + p[:, 1:2] * y1_ref[...].astype(F32)
         + p[:, 2:3] * y2_ref[...].astype(F32) + p[:, 3:4] * y3_ref[...].astype(F32))
    v = alpha * x_ref[...] + (1.0 + g_ref[...]) * y
    xo_ref[...] = _layer_norm_rows(v, lng_ref[...], lnb_ref[...])


def combine_ln(ys, prob, x, mod4, ln_g, ln_b, seq, alpha, deps=(), tm=256):
    t, d = x.shape
    ne = prob.shape[1]
    rows_per_b = seq // tm
    row = lambda i: (i, 0)
    return pl.pallas_call(
        _after(functools.partial(_combine_ln_kernel, alpha=alpha), 9, deps),
        grid=(t // tm,),
        in_specs=[pl.BlockSpec((tm, d), row)] * 4 + [
            pl.BlockSpec((tm, ne), row),
            pl.BlockSpec((tm, d), row),
            pl.BlockSpec((None, None, 1, d), lambda i: (i // rows_per_b, 5, 0, 0)),
            _const_spec((1, d)), _const_spec((1, d)),
        ] + _dep_specs(deps),
        out_specs=pl.BlockSpec((tm, d), row),
        out_shape=jax.ShapeDtypeStruct((t, d), F32),
        compiler_params=_cparams(("parallel",)),
        name="combine_ln",
    )(*ys, prob, x, mod4, ln_g.reshape(1, d), ln_b.reshape(1, d), *deps)


def moe_route(h2, w_router, b_router):
    t, d = h2.shape
    ne = w_router.shape[1]
    te = EXPERT_TILE
    eid, prob, rank, counts2d = router(h2, w_router, b_router)
    counts = counts2d[:, 0]
    padded = ((counts + te - 1) // te) * te
    ends = jnp.cumsum(padded)
    starts = ends - padded
    prob = prob.T
    eid4 = eid[:TOP_K]
    first_slot = jnp.zeros_like(eid4)
    for e in range(ne):
        first_slot = jnp.where(eid4 == e, starts[e], first_slot)
    dest4 = first_slot + rank[:TOP_K]
    n_slots = t * TOP_K + ne * te
    tok = jnp.broadcast_to(jnp.arange(t, dtype=jnp.int32)[None, :], (TOP_K, t))
    src = jnp.zeros((n_slots,), jnp.int32).at[dest4.reshape(-1)].set(
        tok.reshape(-1), unique_indices=True, mode=_IN_BOUNDS)
    n_tiles = n_slots // te
    tile_start = jnp.arange(n_tiles, dtype=jnp.int32) * te
    tile_e = jnp.sum((ends[None, :] <= tile_start[:, None]).astype(jnp.int32), axis=1)
    n_used = (ends[-1] // te).astype(jnp.int32).reshape(1)
    last_e = jnp.sum((ends <= ends[-1] - 1).astype(jnp.int32))
    tile_e = jnp.where(tile_start < ends[-1], tile_e, last_e).astype(jnp.int32)
    return dict(src=src, dest4=dest4, prob=prob, tile_e=tile_e, n_used=n_used, order_token=counts2d)


def kernel(x, c, ada_w, ada_b, ln1_g, ln1_b, ln2_g, ln2_b, rg_w_in, rg_conv_w, rg_conv_b, rg_w_rgate, rg_b_rgate, rg_w_igate, rg_b_igate, rg_lam, rg_w_out, sg_w_in, sg_b_in, sg_ln_g, sg_ln_b, sg_w_sp, sg_b_sp, sg_w_out, sg_b_out, ml_w_in, ml_conv_w, ml_conv_b, ml_w_q, ml_w_k, ml_w_v, ml_w_gates, ml_b_gates, ml_skip, ml_norm_g, ml_w_out, moe_w_router, moe_b_router, moe_w_gate_up, moe_b_gate_up, moe_w_down, moe_b_down):
    bsz, seq, d = x.shape
    depth = ada_w.shape[0]
    alpha = (2 * depth) ** 0.25
    xfull = x.reshape(bsz * seq, d)
    mod = ada_mod(c, ada_w, ada_b)
    zero_bias = jnp.zeros((d,), F32)

    groups = BATCH_GROUPS if bsz % BATCH_GROUPS == 0 else 1
    gb = bsz // groups
    xs = [xfull] * groups
    offs = [g * gb * seq for g in range(groups)]

    for i in range(depth):
        kind, j = i % 3, i // 3
        if kind == 0:
            w_in = rg_w_in[j].astype(BF16)
            w_ri = _rg_gate_weights(rg_w_rgate[j], rg_w_igate[j])
            w_out, b_out = rg_w_out[j].astype(BF16), zero_bias
        elif kind == 1:
            w_in = sg_w_in[j].astype(BF16)
            w_out, b_out = sg_w_out[j].astype(BF16), sg_b_out[j]
        else:
            w_in = ml_w_in[j].astype(BF16)
            wq, wk, wv = (_ml_headwise_weights(w[j]) for w in (ml_w_q, ml_w_k, ml_w_v))
            w_out, b_out = ml_w_out[j].astype(BF16), zero_bias
        mods, routes, x_mid, h_mid = [], [], [], []
        prev = ()
        for g in range(groups):
            mod4 = mod[i, g * gb:(g + 1) * gb].reshape(gb, 6, 1, d)
            xt, off = xs[g], offs[g]
            if kind == 0:
                z = mm_in(xt, mod4, 1, 0, w_in, seq, off, deps=prev)
                a = rg_core(z, rg_conv_w[j], rg_conv_b[j], w_ri, rg_b_rgate[j], rg_b_igate[j], rg_lam[j],
                            gb, seq)
            elif kind == 1:
                u, v = sg_in(xt, mod4, w_in, sg_b_in[j], sg_ln_g[j], sg_ln_b[j], seq, deps=prev)
                a = sg_mix(u, v, sg_w_sp[j], sg_b_sp[j])
            else:
                zz = mm_in(xt, mod4, 1, 0, w_in, seq, off, deps=prev)
                q, k, v, xc, gates = ml_qkv(zz, ml_conv_w[j], ml_conv_b[j], wq, wk, wv,
                                            ml_w_gates[j], ml_b_gates[j], gb, seq)
                a = ml_chunk(q, k, v, xc, zz, gates, gates.T, ml_norm_g[j], ml_skip[j], gb, seq)
            xt, h2 = mm_res_ln(a, w_out, b_out, xt, mod4, 2, 4, 3, ln1_g[i], ln1_b[i], seq, alpha, off)
            route = moe_route(h2, moe_w_router[i], moe_b_router[i])
            prev = (route["order_token"],)
            mods.append(mod4); routes.append(route); x_mid.append(xt); h_mid.append(h2)
        ysorted = []
        for g in range(groups):
            r = routes[g]
            rows = h_mid[g].at[r["src"]].get(mode=_IN_BOUNDS)
            ysorted.append(experts(rows, r["tile_e"], r["n_used"], moe_w_gate_up, moe_b_gate_up,
                                   moe_w_down, moe_b_down, i, deps=prev))
            prev = (ysorted[g],)
        for g in range(groups):
            r = routes[g]
            ys = [ysorted[g].at[r["dest4"][k]].get(mode=_IN_BOUNDS) for k in range(TOP_K)]
            xs[g] = combine_ln(ys, r["prob"], x_mid[g], mods[g], ln2_g[i], ln2_b[i], seq, alpha, deps=prev)
            prev = (xs[g],)
            offs[g] = 0
    return jnp.concatenate(xs, axis=0).reshape(bsz, seq, d)
```

```python
import functools
import math

import jax
import jax.numpy as jnp
from jax import lax
from jax.experimental import pallas as pl
from jax.experimental.pallas import tpu as pltpu

F32 = jnp.float32
BF16 = jnp.bfloat16

LN_EPS = 1e-5
RG_C = 8.0
CONV_W = 4
CHUNK = 64
SG_BLOCK = 128
SG_GROUPS = 8
ML_HEADS = 8
TOP_K = 4
SWIGLU_LIMIT = 7.0
SWIGLU_ALPHA = 1.702

SUBLANES = 8
RG_SUPER = 640
ML_L = 256
EXPERT_TILE = 512
BATCH_GROUPS = 2
VMEM_LIMIT = 56 * 1024 * 1024
NEG = -1e30
_IN_BOUNDS = "promise_in_bounds"


def _cparams(sem):
    return pltpu.CompilerParams(dimension_semantics=sem, vmem_limit_bytes=VMEM_LIMIT)


def _sigmoid(x):
    return 1.0 / (1.0 + jnp.exp(-x))


def _gelu_tanh(x):
    return 0.5 * x * (1.0 + jnp.tanh(math.sqrt(2.0 / math.pi) * (x + 0.044715 * (x * x * x))))


def _softplus(x):
    return jnp.maximum(x, 0.0) + jnp.log1p(jnp.exp(-jnp.abs(x)))


def _layer_norm_rows(v, g, b):
    mu = jnp.mean(v, axis=-1, keepdims=True)
    d = v - mu
    var = jnp.mean(d * d, axis=-1, keepdims=True)
    return d * lax.rsqrt(var + LN_EPS) * g + b


def _const_spec(shape):
    nd = len(shape)
    return pl.BlockSpec(shape, lambda *_: (0,) * nd, pipeline_mode=pl.Buffered(1))


def _after(kernel_fn, n_in, deps):
    nd = len(deps)
    if nd == 0:
        return kernel_fn

    def body(*refs):
        return kernel_fn(*refs[:n_in], *refs[n_in + nd:])

    return body


def _dep_specs(deps):
    return [pl.BlockSpec(memory_space=pl.ANY)] * len(deps)


def _ada_kernel(c_ref, w_ref, b_ref, o_ref):
    c = c_ref[...]
    cond = c * _sigmoid(c)
    o_ref[...] = jnp.dot(cond.astype(BF16), w_ref[...].astype(BF16),
                         preferred_element_type=F32) + b_ref[...]


def ada_mod(c, ada_w, ada_b, tn=1024):
    depth, d, n = ada_w.shape
    bsz = c.shape[0]
    return pl.pallas_call(
        _ada_kernel,
        grid=(depth, n // tn),
        in_specs=[
            pl.BlockSpec((bsz, d), lambda l, j: (0, 0)),
            pl.BlockSpec((None, d, tn), lambda l, j: (l, 0, j)),
            pl.BlockSpec((None, 1, tn), lambda l, j: (l, 0, j)),
        ],
        out_specs=pl.BlockSpec((None, bsz, tn), lambda l, j: (l, 0, j)),
        out_shape=jax.ShapeDtypeStruct((depth, bsz, n), F32),
        compiler_params=_cparams(("parallel", "parallel")),
        name="ada_mod",
    )(c, ada_w, ada_b.reshape(depth, 1, n))


def _mm_in_kernel(x_ref, sc_ref, sh_ref, w_ref, o_ref, *, tn):
    h = (x_ref[...] * (1.0 + sc_ref[...]) + sh_ref[...]).astype(BF16)
    for j in range(o_ref.shape[1] // tn):
        cols = slice(j * tn, (j + 1) * tn)
        o_ref[:, cols] = jnp.dot(h, w_ref[:, cols], preferred_element_type=F32).astype(o_ref.dtype)


def mm_in(x, mod4, k_sc, k_sh, w, seq, row_off=0, deps=(), tm=512, tn=1024):
    t, d = mod4.shape[0] * seq, x.shape[1]
    n = w.shape[1]
    tm = min(tm, seq)
    rows_per_b = seq // tm
    blk_off = row_off // tm
    return pl.pallas_call(
        _after(functools.partial(_mm_in_kernel, tn=tn), 4, deps),
        grid=(t // tm,),
        in_specs=[
            pl.BlockSpec((tm, d), lambda i: (i + blk_off, 0)),
            pl.BlockSpec((None, None, 1, d), lambda i: (i // rows_per_b, k_sc, 0, 0)),
            pl.BlockSpec((None, None, 1, d), lambda i: (i // rows_per_b, k_sh, 0, 0)),
            _const_spec((d, n)),
        ] + _dep_specs(deps),
        out_specs=pl.BlockSpec((tm, n), lambda i: (i, 0)),
        out_shape=jax.ShapeDtypeStruct((t, n), BF16),
        compiler_params=_cparams(("parallel",)),
        name="mm_in",
    )(x, mod4, mod4, w, *deps)


def _mm_res_ln_kernel(a_ref, w_ref, b_ref, x_ref, g_ref, lng_ref, lnb_ref, sc_ref, sh_ref,
                      xo_ref, ho_ref, *, alpha):
    y = jnp.dot(a_ref[...], w_ref[...], preferred_element_type=F32) + b_ref[...]
    v = alpha * x_ref[...] + (1.0 + g_ref[...]) * y
    xn = _layer_norm_rows(v, lng_ref[...], lnb_ref[...])
    xo_ref[...] = xn
    ho_ref[...] = (xn * (1.0 + sc_ref[...]) + sh_ref[...]).astype(BF16)


def mm_res_ln(a, w, bias, x, mod4, k_gate, k_sc, k_sh, ln_g, ln_b, seq, alpha, row_off=0, tm=256):
    t, k = a.shape
    d = x.shape[1]
    rows_per_b = seq // tm
    blk_off = row_off // tm
    modspec = lambda kk: pl.BlockSpec((None, None, 1, d), lambda i: (i // rows_per_b, kk, 0, 0))
    return pl.pallas_call(
        functools.partial(_mm_res_ln_kernel, alpha=alpha),
        grid=(t // tm,),
        in_specs=[
            pl.BlockSpec((tm, k), lambda i: (i, 0)),
            _const_spec((k, d)),
            _const_spec((1, d)),
            pl.BlockSpec((tm, d), lambda i: (i + blk_off, 0)),
            modspec(k_gate),
            _const_spec((1, d)),
            _const_spec((1, d)),
            modspec(k_sc),
            modspec(k_sh),
        ],
        out_specs=[pl.BlockSpec((tm, d), lambda i: (i, 0)), pl.BlockSpec((tm, d), lambda i: (i, 0))],
        out_shape=[jax.ShapeDtypeStruct((t, d), F32), jax.ShapeDtypeStruct((t, d), BF16)],
        compiler_params=_cparams(("parallel",)),
        name="mm_res_ln",
    )(a, w, bias.reshape(1, d), x, mod4, ln_g.reshape(1, d), ln_b.reshape(1, d), mod4, mod4)


def _causal_conv(buf_ref, x, w_ref, b_ref, first):
    ts = x.shape[0]

    @pl.when(first)
    def _():
        buf_ref[0:SUBLANES, :] = jnp.zeros((SUBLANES, x.shape[1]), F32)

    buf_ref[SUBLANES:SUBLANES + ts, :] = x
    y = b_ref[...] + w_ref[CONV_W - 1:CONV_W, :] * x
    for back in range(1, CONV_W):
        y = y + w_ref[CONV_W - 1 - back:CONV_W - back, :] * buf_ref[SUBLANES - back:SUBLANES - back + ts, :]
    buf_ref[0:SUBLANES, :] = buf_ref[ts:ts + SUBLANES, :]
    return y


def _scan_rows(a, u, h0):
    ts, c = a.shape
    groups = ts // SUBLANES
    a3 = a.reshape(groups, SUBLANES, c)
    u3 = u.reshape(groups, SUBLANES, c)
    rid = lax.broadcasted_iota(jnp.int32, a3.shape, 1)
    for k in (1, 2, 4):
        keep = rid >= k
        a_sh = jnp.where(keep, pltpu.roll(a3, k, 1), 1.0)
        u_sh = jnp.where(keep, pltpu.roll(u3, k, 1), 0.0)
        u3 = u3 + a3 * u_sh
        a3 = a3 * a_sh
    h = h0
    outs = []
    for j in range(groups):
        hj = u3[j] + a3[j] * h
        outs.append(hj)
        h = hj[SUBLANES - 1:SUBLANES, :]
    return jnp.concatenate(outs, axis=0), h


def _rg_core_kernel(gate_ref, rec_ref, cw_ref, cb_ref, wri_ref, br_ref, bi_ref, lam_ref,
                    o_ref, buf_ref, h_ref):
    first = pl.program_id(2) == 0
    c = o_ref.shape[1]
    xr = _causal_conv(buf_ref, rec_ref[...].astype(F32), cw_ref, cb_ref, first)
    pre = jnp.dot(xr.astype(BF16), wri_ref[...], preferred_element_type=F32)
    r = _sigmoid(pre[:, :c] + br_ref[...])
    i = _sigmoid(pre[:, c:] + bi_ref[...])
    log_a = (-RG_C) * r * _softplus(-lam_ref[...])
    a = jnp.exp(log_a)
    u = jnp.sqrt(-jnp.tanh(log_a) * (a * a + 1.0)) * (i * xr)

    @pl.when(first)
    def _():
        h_ref[...] = jnp.zeros_like(h_ref)

    hs, h_last = _scan_rows(a, u, h_ref[...])
    h_ref[...] = h_last
    o_ref[...] = (_gelu_tanh(gate_ref[...].astype(F32)) * hs).astype(BF16)


def rg_core(z, conv_w, conv_b, w_ri, b_r, b_i, lam, bsz, seq, ts=256):
    t = z.shape[0]
    width = conv_w.shape[1]
    nsup = width // RG_SUPER
    spb = seq // ts
    row = lambda c, b, s: b * spb + s
    vec = lambda: pl.BlockSpec((1, RG_SUPER), lambda c, b, s: (0, c))
    return pl.pallas_call(
        _rg_core_kernel,
        grid=(nsup, bsz, spb),
        in_specs=[
            pl.BlockSpec((ts, RG_SUPER), lambda c, b, s: (row(c, b, s), c)),
            pl.BlockSpec((ts, RG_SUPER), lambda c, b, s: (row(c, b, s), nsup + c)),
            pl.BlockSpec((CONV_W, RG_SUPER), lambda c, b, s: (0, c)),
            vec(),
            pl.BlockSpec((None, RG_SUPER, 2 * RG_SUPER), lambda c, b, s: (c, 0, 0)),
            vec(), vec(), vec(),
        ],
        out_specs=pl.BlockSpec((ts, RG_SUPER), lambda c, b, s: (row(c, b, s), c)),
        out_shape=jax.ShapeDtypeStruct((t, width), BF16),
        scratch_shapes=[pltpu.VMEM((ts + SUBLANES, RG_SUPER), F32), pltpu.VMEM((1, RG_SUPER), F32)],
        compiler_params=_cparams(("parallel", "parallel", "arbitrary")),
        name="rg_core",
    )(z, z, conv_w, conv_b.reshape(1, width), w_ri, b_r.reshape(1, width), b_i.reshape(1, width),
      lam.reshape(1, width))


def _rg_gate_weights(w_r, w_i):
    nb, bs, _ = w_r.shape
    per = RG_SUPER // bs
    nsup = nb // per

    def dense(w):
        w4 = w.reshape(nsup, per, bs, bs)
        eye = jnp.eye(per, dtype=w.dtype)
        return jnp.einsum("spcd,pq->spcqd", w4, eye).reshape(nsup, RG_SUPER, RG_SUPER)

    return jnp.concatenate([dense(w_r), dense(w_i)], axis=-1).astype(BF16)


def _sg_in_kernel(x_ref, sc_ref, sh_ref, w_ref, b_ref, lng_ref, lnb_ref, u_ref, v_ref):
    half = u_ref.shape[1]
    h = (x_ref[...] * (1.0 + sc_ref[...]) + sh_ref[...]).astype(BF16)
    zu = jnp.dot(h, w_ref[:, :half], preferred_element_type=F32) + b_ref[:, :half]
    u_ref[...] = _gelu_tanh(zu).astype(BF16)
    zv = jnp.dot(h, w_ref[:, half:], preferred_element_type=F32) + b_ref[:, half:]
    v_ref[...] = _layer_norm_rows(_gelu_tanh(zv), lng_ref[...], lnb_ref[...]).astype(BF16)


def sg_in(x, mod4, w, b, ln_g, ln_b, seq, deps=(), tm=256):
    t, d = x.shape
    half = w.shape[1] // 2
    rows_per_b = seq // tm
    modspec = lambda kk: pl.BlockSpec((None, None, 1, d), lambda i: (i // rows_per_b, kk, 0, 0))
    row = lambda i: (i, 0)
    return pl.pallas_call(
        _after(_sg_in_kernel, 7, deps),
        grid=(t // tm,),
        in_specs=[
            pl.BlockSpec((tm, d), row),
            modspec(1), modspec(0),
            _const_spec((d, 2 * half)),
            _const_spec((1, 2 * half)),
            _const_spec((1, half)), _const_spec((1, half)),
        ] + _dep_specs(deps),
        out_specs=[pl.BlockSpec((tm, half), row), pl.BlockSpec((tm, half), row)],
        out_shape=[jax.ShapeDtypeStruct((t, half), BF16), jax.ShapeDtypeStruct((t, half), BF16)],
        compiler_params=_cparams(("parallel",)),
        name="sg_in",
    )(x, mod4, mod4, w, b.reshape(1, 2 * half), ln_g.reshape(1, half), ln_b.reshape(1, half), *deps)


def _sg_mix_kernel(u_ref, v_ref, w_ref, bt_ref, o_ref):
    nblk = u_ref.shape[0] // SG_BLOCK
    gw = u_ref.shape[1] // SG_GROUPS
    ii = lax.broadcasted_iota(jnp.int32, (SG_BLOCK, SG_BLOCK), 0)
    jj = lax.broadcasted_iota(jnp.int32, (SG_BLOCK, SG_BLOCK), 1)
    allowed = (jj // CHUNK) <= (ii // CHUNK)
    for g in range(SG_GROUPS):
        w = jnp.where(allowed, w_ref[g], 0.0).astype(BF16)
        bias = bt_ref[:, g:g + 1]
        for n in range(nblk):
            rows = slice(n * SG_BLOCK, (n + 1) * SG_BLOCK)
            cols = slice(g * gw, (g + 1) * gw)
            mixed = jnp.dot(w, v_ref[rows, cols], preferred_element_type=F32) + bias
            o_ref[rows, cols] = (u_ref[rows, cols].astype(F32) * mixed).astype(BF16)


def sg_mix(u, v, w_sp, b_sp, tm=256):
    t, half = u.shape
    return pl.pallas_call(
        _sg_mix_kernel,
        grid=(t // tm,),
        in_specs=[
            pl.BlockSpec((tm, half), lambda i: (i, 0)),
            pl.BlockSpec((tm, half), lambda i: (i, 0)),
            _const_spec(w_sp.shape),
            _const_spec((SG_BLOCK, SG_GROUPS)),
        ],
        out_specs=pl.BlockSpec((tm, half), lambda i: (i, 0)),
        out_shape=jax.ShapeDtypeStruct((t, half), BF16),
        compiler_params=_cparams(("parallel",)),
        name="sg_mix",
    )(u, v, w_sp, b_sp.T)


def _ml_qkv_kernel(xm_ref, cw_ref, cb_ref, wq_ref, wk_ref, wv_ref, wg_ref, bg_ref,
                   q_ref, k_ref, v_ref, xc_ref, g_ref, buf_ref):
    first = pl.program_id(1) == 0
    width = xm_ref.shape[1]
    hd = width // ML_HEADS
    xm = xm_ref[...]
    conv = _causal_conv(buf_ref, xm.astype(F32), cw_ref, cb_ref, first)
    xc = (conv * _sigmoid(conv)).astype(BF16)
    xc_ref[...] = xc
    for h in range(ML_HEADS):
        cols = slice(h * hd, (h + 1) * hd)
        q_ref[:, cols] = jnp.dot(xc[:, cols], wq_ref[h], preferred_element_type=F32).astype(BF16)
        k_ref[:, cols] = jnp.dot(xc[:, cols], wk_ref[h], preferred_element_type=F32).astype(BF16)
        v_ref[:, cols] = jnp.dot(xm[:, cols], wv_ref[h], preferred_element_type=F32).astype(BF16)
    gates = (jnp.dot(q_ref[...], wg_ref[0], preferred_element_type=F32)
             + jnp.dot(k_ref[...], wg_ref[1], preferred_element_type=F32)
             + jnp.dot(v_ref[...], wg_ref[2], preferred_element_type=F32) + bg_ref[...])
    is_f = lax.broadcasted_iota(jnp.int32, gates.shape, 1) >= ML_HEADS
    g_ref[...] = jnp.where(is_f, -_softplus(-gates), gates)


def _ml_headwise_weights(w):
    nqb, bs, _ = w.shape
    per = nqb // ML_HEADS
    w4 = w.reshape(ML_HEADS, per, bs, bs)
    eye = jnp.eye(per, dtype=w.dtype)
    return jnp.einsum("hpcd,pq->hpcqd", w4, eye).reshape(ML_HEADS, per * bs, per * bs).astype(BF16)


def ml_qkv(zz, conv_w, conv_b, wq, wk, wv, w_gates, b_gates, bsz, seq, ts=256):
    t = zz.shape[0]
    width = conv_w.shape[1]
    hd = width // ML_HEADS
    spb = seq // ts
    row = lambda b, s: (b * spb + s, 0)
    act = lambda: pl.BlockSpec((ts, width), row)
    return pl.pallas_call(
        _ml_qkv_kernel,
        grid=(bsz, spb),
        in_specs=[
            act(),
            _const_spec((CONV_W, width)), _const_spec((1, width)),
            _const_spec((ML_HEADS, hd, hd)), _const_spec((ML_HEADS, hd, hd)), _const_spec((ML_HEADS, hd, hd)),
            _const_spec((3, width, 2 * ML_HEADS)), _const_spec((1, 2 * ML_HEADS)),
        ],
        out_specs=[act(), act(), act(), act(), pl.BlockSpec((ts, 2 * ML_HEADS), row)],
        out_shape=[jax.ShapeDtypeStruct((t, width), BF16)] * 4
                  + [jax.ShapeDtypeStruct((t, 2 * ML_HEADS), F32)],
        scratch_shapes=[pltpu.VMEM((ts + SUBLANES, width), F32)],
        compiler_params=_cparams(("parallel", "arbitrary")),
        name="ml_qkv",
    )(zz, conv_w, conv_b.reshape(1, width), wq, wk, wv,
      w_gates.reshape(3, width, 2 * ML_HEADS).astype(BF16), b_gates.reshape(1, 2 * ML_HEADS))


def _ml_chunk_kernel(q_ref, k_ref, v_ref, xc_ref, z_ref, gc_ref, gr_ref, ng_ref, skip_ref,
                     o_ref, c_scr, n_scr, m_scr):
    L, width = q_ref.shape
    hd = width // ML_HEADS
    scale = hd ** -0.5

    @pl.when(pl.program_id(1) == 0)
    def _():
        c_scr[...] = jnp.zeros_like(c_scr)
        n_scr[...] = jnp.zeros_like(n_scr)
        m_scr[...] = jnp.zeros_like(m_scr)

    ri = lax.broadcasted_iota(jnp.int32, (L, L), 0)
    ci = lax.broadcasted_iota(jnp.int32, (L, L), 1)
    causal = ri >= ci
    lower = causal.astype(F32)
    upper = (ri <= ci).astype(F32)
    gc = gc_ref[...]
    gr = gr_ref[...]
    bcol = jnp.dot(lower, gc[:, ML_HEADS:], preferred_element_type=F32, precision=lax.Precision.HIGHEST)
    brow = jnp.dot(gr[ML_HEADS:, :], upper, preferred_element_type=F32, precision=lax.Precision.HIGHEST)

    for h in range(ML_HEADS):
        cols = slice(h * hd, (h + 1) * hd)
        bc = bcol[:, h:h + 1]
        br = brow[h:h + 1, :]
        lir = gr[h:h + 1, :]
        lic = gc[:, h:h + 1]
        m_prev = m_scr[h:h + 1, 0:1]
        g = bc + m_prev
        dmat = jnp.where(causal, bc - br + lir, NEG)
        m_t = jnp.maximum(g, jnp.max(dmat, axis=-1, keepdims=True))
        w_intra = jnp.exp(dmat - m_t)
        w_inter = jnp.exp(g - m_t)
        qh = q_ref[:, cols]
        kh = k_ref[:, cols]
        vh = v_ref[:, cols]
        s = lax.dot_general(qh, kh, (((1,), (1,)), ((), ())), preferred_element_type=F32) * scale * w_intra
        c_old = c_scr[h]
        n_old = n_scr[h:h + 1, :]
        num = (jnp.dot(s.astype(BF16), vh, preferred_element_type=F32)
               + w_inter * jnp.dot(qh, c_old.astype(BF16), preferred_element_type=F32))
        den = (jnp.sum(s, axis=-1, keepdims=True)
               + w_inter * jnp.sum(qh.astype(F32) * n_old, axis=-1, keepdims=True))
        out = num / jnp.maximum(jnp.abs(den), jnp.exp(-m_t))

        b_last = bc[L - 1:L, :]
        dec = b_last - bc + lic
        m_new = jnp.maximum(b_last + m_prev, jnp.max(dec, axis=0, keepdims=True))
        ws = jnp.exp(dec - m_new)
        wc = jnp.exp(b_last + m_prev - m_new)
        kw = kh.astype(F32) * (ws * scale)
        c_scr[h] = wc * c_old + lax.dot_general(kw.astype(BF16), vh, (((0,), (0,)), ((), ())),
                                                preferred_element_type=F32)
        n_scr[h:h + 1, :] = wc * n_old + jnp.sum(kw, axis=0, keepdims=True)
        m_scr[h:h + 1, :] = jnp.broadcast_to(m_new, (1, m_scr.shape[1]))

        mu = jnp.mean(out, axis=-1, keepdims=True)
        dlt = out - mu
        var = jnp.mean(dlt * dlt, axis=-1, keepdims=True)
        hn = dlt * lax.rsqrt(var + LN_EPS) * ng_ref[:, cols]
        hn = hn + skip_ref[:, cols] * xc_ref[:, cols].astype(F32)
        o_ref[:, cols] = (_sigmoid(z_ref[:, cols].astype(F32)) * hn).astype(BF16)


def ml_chunk(q, k, v, xc, zz, gates, gates_t, norm_g, skip, bsz, seq):
    t, width = q.shape
    hd = width // ML_HEADS
    L = ML_L
    cpb = seq // L
    row = lambda b, c: (b * cpb + c, 0)
    act = lambda: pl.BlockSpec((L, width), row)
    return pl.pallas_call(
        _ml_chunk_kernel,
        grid=(bsz, cpb),
        in_specs=[
            act(), act(), act(), act(),
            pl.BlockSpec((L, width), lambda b, c: (b * cpb + c, 1)),
            pl.BlockSpec((L, 2 * ML_HEADS), row),
            pl.BlockSpec((2 * ML_HEADS, L), lambda b, c: (0, b * cpb + c)),
            _const_spec((1, width)), _const_spec((1, width)),
        ],
        out_specs=act(),
        out_shape=jax.ShapeDtypeStruct((t, width), BF16),
        scratch_shapes=[pltpu.VMEM((ML_HEADS, hd, hd), F32), pltpu.VMEM((ML_HEADS, hd), F32),
                        pltpu.VMEM((ML_HEADS, 128), F32)],
        compiler_params=_cparams(("parallel", "arbitrary")),
        name="ml_chunk",
    )(q, k, v, xc, zz, gates, gates_t, norm_g.reshape(1, width), skip.reshape(1, width))


def _router_kernel(h_ref, w_ref, b_ref, eid_ref, prob_ref, rank_ref, cnt_ref, cnt_scr):
    tr = h_ref.shape[0]
    ne = w_ref.shape[0]

    @pl.when(pl.program_id(0) == 0)
    def _():
        cnt_scr[...] = jnp.zeros_like(cnt_scr)

    logits = lax.dot_general(w_ref[...], h_ref[...], (((1,), (1,)), ((), ())),
                             preferred_element_type=F32) + b_ref[...]
    sub = lax.broadcasted_iota(jnp.int32, (ne, tr), 0)
    work = logits
    sel = jnp.zeros((ne, tr), F32)
    hits = []
    top = None
    for k in range(TOP_K):
        m = jnp.max(work, axis=0, keepdims=True)
        idx = jnp.min(jnp.where(work == m, sub, ne), axis=0, keepdims=True)
        hit = sub == idx
        hits.append((hit, idx))
        sel = jnp.where(hit, 1.0, sel)
        work = jnp.where(hit, -jnp.inf, work)
        if k == 0:
            top = m
    e = sel * jnp.exp(logits - top)
    p = e / jnp.sum(e, axis=0, keepdims=True)

    ri = lax.broadcasted_iota(jnp.int32, (tr, tr), 0)
    ci = lax.broadcasted_iota(jnp.int32, (tr, tr), 1)
    strict_upper = (ri < ci).astype(BF16)
    before = jnp.dot(sel.astype(BF16), strict_upper, preferred_element_type=F32)
    rank = cnt_scr[...] + before
    cnt_scr[...] = cnt_scr[...] + jnp.sum(sel, axis=1, keepdims=True)
    cnt_ref[...] = cnt_scr[...].astype(jnp.int32)

    row = lax.broadcasted_iota(jnp.int32, (SUBLANES, tr), 0)
    eid = jnp.zeros((SUBLANES, tr), jnp.int32)
    prob = jnp.zeros((SUBLANES, tr), F32)
    rnk = jnp.zeros((SUBLANES, tr), F32)
    for k, (hit, idx) in enumerate(hits):
        slot = row == k
        eid = jnp.where(slot, idx, eid)
        prob = jnp.where(slot, jnp.sum(jnp.where(hit, p, 0.0), axis=0, keepdims=True), prob)
        rnk = jnp.where(slot, jnp.sum(jnp.where(hit, rank, 0.0), axis=0, keepdims=True), rnk)
    eid_ref[...] = eid
    prob_ref[...] = prob
    rank_ref[...] = rnk.astype(jnp.int32)


def router(h2, w_router, b_router, tr=512):
    t, d = h2.shape
    ne = w_router.shape[1]
    tr = min(tr, t)
    col = lambda i: (0, i)
    return pl.pallas_call(
        _router_kernel,
        grid=(t // tr,),
        in_specs=[pl.BlockSpec((tr, d), lambda i: (i, 0)), _const_spec((ne, d)), _const_spec((ne, 1))],
        out_specs=[pl.BlockSpec((SUBLANES, tr), col), pl.BlockSpec((SUBLANES, tr), col),
                   pl.BlockSpec((SUBLANES, tr), col), pl.BlockSpec((ne, 1), lambda i: (0, 0))],
        out_shape=[jax.ShapeDtypeStruct((SUBLANES, t), jnp.int32), jax.ShapeDtypeStruct((SUBLANES, t), F32),
                   jax.ShapeDtypeStruct((SUBLANES, t), jnp.int32), jax.ShapeDtypeStruct((ne, 1), jnp.int32)],
        scratch_shapes=[pltpu.VMEM((ne, 1), F32)],
        compiler_params=_cparams(("arbitrary",)),
        name="router",
    )(h2, w_router.T.astype(BF16), b_router.reshape(ne, 1))


def _expert_kernel(te_ref, nu_ref, nx_ref, x_ref, wgu_hbm, bgu_ref, wd_hbm, bd_ref, o_ref,
                   wgu_f32, wd_f32, wgu_bf, wd_bf, sem, *, layer):
    i = pl.program_id(0)
    used = i < nu_ref[0]
    e = te_ref[i]
    new_expert = jnp.logical_or(i == 0, e != te_ref[jnp.maximum(i - 1, 0)])

    def fetch(expert):
        return (pltpu.make_async_copy(wgu_hbm.at[layer, expert], wgu_f32, sem.at[0]),
                pltpu.make_async_copy(wd_hbm.at[layer, expert], wd_f32, sem.at[1]))

    @pl.when(jnp.logical_and(used, i == 0))
    def _():
        for cp in fetch(e):
            cp.start()

    @pl.when(jnp.logical_and(used, new_expert))
    def _():
        for cp in fetch(e):
            cp.wait()
        wgu_bf[...] = wgu_f32[...].astype(BF16)
        wd_bf[...] = wd_f32[...].astype(BF16)
        nxt = nx_ref[i]

        @pl.when(nxt != e)
        def _():
            for cp in fetch(nxt):
                cp.start()

    @pl.when(used)
    def _():
        f = wd_bf.shape[0]
        gu = jnp.dot(x_ref[...], wgu_bf[...], preferred_element_type=F32) + bgu_ref[...]
        gate = jnp.minimum(gu[:, :f], SWIGLU_LIMIT)
        up = jnp.clip(gu[:, f:], -SWIGLU_LIMIT, SWIGLU_LIMIT)
        act = gate * _sigmoid(SWIGLU_ALPHA * gate) * (up + 1.0)
        y = jnp.dot(act.astype(BF16), wd_bf[...], preferred_element_type=F32) + bd_ref[...]
        o_ref[...] = y.astype(BF16)

    @pl.when(jnp.logical_not(used))
    def _():
        o_ref[...] = jnp.zeros_like(o_ref)


def experts(xs, tile_e, n_used, tile_next, w_gu, b_gu, w_d, b_d, layer, deps=()):
    p, d = xs.shape
    depth, ne, _, f2 = w_gu.shape
    f = f2 // 2
    te = EXPERT_TILE
    grid_spec = pltpu.PrefetchScalarGridSpec(
        num_scalar_prefetch=3,
        grid=(p // te,),
        in_specs=[
            pl.BlockSpec((te, d), lambda i, e, n, nx: (i, 0)),
            pl.BlockSpec(memory_space=pl.ANY),
            pl.BlockSpec((None, None, 1, f2), lambda i, e, n, nx: (layer, e[i], 0, 0)),
            pl.BlockSpec(memory_space=pl.ANY),
            pl.BlockSpec((None, None, 1, d), lambda i, e, n, nx: (layer, e[i], 0, 0)),
        ] + _dep_specs(deps),
        out_specs=pl.BlockSpec((te, d), lambda i, e, n, nx: (i, 0)),
        scratch_shapes=[pltpu.VMEM((d, f2), F32), pltpu.VMEM((f, d), F32),
                        pltpu.VMEM((d, f2), BF16), pltpu.VMEM((f, d), BF16),
                        pltpu.SemaphoreType.DMA((2,))],
    )
    return pl.pallas_call(
        _after(functools.partial(_expert_kernel, layer=layer), 8, deps),
        grid_spec=grid_spec,
        out_shape=jax.ShapeDtypeStruct((p, d), BF16),
        compiler_params=_cparams(("arbitrary",)),
        name="experts",
    )(tile_e, n_used, tile_next, xs, w_gu, b_gu.reshape(depth, ne, 1, f2), w_d,
      b_d.reshape(depth, ne, 1, d), *deps)


def _combine_ln_kernel(y0_ref, y1_ref, y2_ref, y3_ref, p_ref, x_ref, g_ref, lng_ref, lnb_ref, xo_ref,
                       *, alpha):
    p = p_ref[...]
    y = (p[:, 0:1] * y0_ref[...].astype(F32) + p[:, 1:2] * y1_ref[...].astype(F32)
         + p[:, 2:3] * y2_ref[...].astype(F32) + p[:, 3:4] * y3_ref[...].astype(F32))
    v = alpha * x_ref[...] + (1.0 + g_ref[...]) * y
    xo_ref[...] = _layer_norm_rows(v, lng_ref[...], lnb_ref[...])


def combine_ln(ys, prob, x, mod4, ln_g, ln_b, seq, alpha, deps=(), tm=256):
    t, d = x.shape
    ne = prob.shape[1]
    rows_per_b = seq // tm
    row = lambda i: (i, 0)
    return pl.pallas_call(
        _after(functools.partial(_combine_ln_kernel, alpha=alpha), 9, deps),
        grid=(t // tm,),
        in_specs=[pl.BlockSpec((tm, d), row)] * 4 + [
            pl.BlockSpec((tm, ne), row),
            pl.BlockSpec((tm, d), row),
            pl.BlockSpec((None, None, 1, d), lambda i: (i // rows_per_b, 5, 0, 0)),
            _const_spec((1, d)), _const_spec((1, d)),
        ] + _dep_specs(deps),
        out_specs=pl.BlockSpec((tm, d), row),
        out_shape=jax.ShapeDtypeStruct((t, d), F32),
        compiler_params=_cparams(("parallel",)),
        name="combine_ln",
    )(*ys, prob, x, mod4, ln_g.reshape(1, d), ln_b.reshape(1, d), *deps)


def moe_route(h2, w_router, b_router):
    t, d = h2.shape
    ne = w_router.shape[1]
    te = EXPERT_TILE
    eid, prob, rank, counts2d = router(h2, w_router, b_router)
    counts = counts2d[:, 0]
    padded = ((counts + te - 1) // te) * te
    ends = jnp.cumsum(padded)
    starts = ends - padded
    prob = prob.T
    eid4 = eid[:TOP_K]
    first_slot = jnp.zeros_like(eid4)
    for e in range(ne):
        first_slot = jnp.where(eid4 == e, starts[e], first_slot)
    dest4 = first_slot + rank[:TOP_K]
    n_slots = t * TOP_K + ne * te
    tok = jnp.broadcast_to(jnp.arange(t, dtype=jnp.int32)[None, :], (TOP_K, t))
    src = jnp.zeros((n_slots,), jnp.int32).at[dest4.reshape(-1)].set(
        tok.reshape(-1), unique_indices=True, mode=_IN_BOUNDS)
    n_tiles = n_slots // te
    tile_start = jnp.arange(n_tiles, dtype=jnp.int32) * te
    tile_e = jnp.sum((ends[None, :] <= tile_start[:, None]).astype(jnp.int32), axis=1)
    n_used = (ends[-1] // te).astype(jnp.int32).reshape(1)
    last_e = jnp.sum((ends <= ends[-1] - 1).astype(jnp.int32))
    tile_e = jnp.where(tile_start < ends[-1], tile_e, last_e).astype(jnp.int32)
    ids = jnp.arange(ne, dtype=jnp.int32)
    later = jnp.where((ids[None, :] > ids[:, None]) & (padded[None, :] > 0), ids[None, :], ne)
    nxt = jnp.min(later, axis=1)
    nxt = jnp.where(nxt == ne, ids, nxt)
    tile_next = jnp.sum(jnp.where(tile_e[:, None] == ids[None, :], nxt[None, :], 0), axis=1).astype(jnp.int32)
    return dict(src=src, dest4=dest4, prob=prob, tile_e=tile_e, n_used=n_used, tile_next=tile_next,
                order_token=counts2d)


def kernel(x, c, ada_w, ada_b, ln1_g, ln1_b, ln2_g, ln2_b, rg_w_in, rg_conv_w, rg_conv_b, rg_w_rgate, rg_b_rgate, rg_w_igate, rg_b_igate, rg_lam, rg_w_out, sg_w_in, sg_b_in, sg_ln_g, sg_ln_b, sg_w_sp, sg_b_sp, sg_w_out, sg_b_out, ml_w_in, ml_conv_w, ml_conv_b, ml_w_q, ml_w_k, ml_w_v, ml_w_gates, ml_b_gates, ml_skip, ml_norm_g, ml_w_out, moe_w_router, moe_b_router, moe_w_gate_up, moe_b_gate_up, moe_w_down, moe_b_down):
    bsz, seq, d = x.shape
    depth = ada_w.shape[0]
    alpha = (2 * depth) ** 0.25
    xfull = x.reshape(bsz * seq, d)
    mod = ada_mod(c, ada_w, ada_b)
    zero_bias = jnp.zeros((d,), F32)

    groups = BATCH_GROUPS if bsz % BATCH_GROUPS == 0 else 1
    gb = bsz // groups
    xs = [xfull] * groups
    offs = [g * gb * seq for g in range(groups)]

    for i in range(depth):
        kind, j = i % 3, i // 3
        if kind == 0:
            w_in = rg_w_in[j].astype(BF16)
            w_ri = _rg_gate_weights(rg_w_rgate[j], rg_w_igate[j])
            w_out, b_out = rg_w_out[j].astype(BF16), zero_bias
        elif kind == 1:
            w_in = sg_w_in[j].astype(BF16)
            w_out, b_out = sg_w_out[j].astype(BF16), sg_b_out[j]
        else:
            w_in = ml_w_in[j].astype(BF16)
            wq, wk, wv = (_ml_headwise_weights(w[j]) for w in (ml_w_q, ml_w_k, ml_w_v))
            w_out, b_out = ml_w_out[j].astype(BF16), zero_bias
        mods, routes, x_mid, h_mid = [], [], [], []
        prev = ()
        for g in range(groups):
            mod4 = mod[i, g * gb:(g + 1) * gb].reshape(gb, 6, 1, d)
            xt, off = xs[g], offs[g]
            if kind == 0:
                z = mm_in(xt, mod4, 1, 0, w_in, seq, off, deps=prev)
                a = rg_core(z, rg_conv_w[j], rg_conv_b[j], w_ri, rg_b_rgate[j], rg_b_igate[j], rg_lam[j],
                            gb, seq)
            elif kind == 1:
                u, v = sg_in(xt, mod4, w_in, sg_b_in[j], sg_ln_g[j], sg_ln_b[j], seq, deps=prev)
                a = sg_mix(u, v, sg_w_sp[j], sg_b_sp[j])
            else:
                zz = mm_in(xt, mod4, 1, 0, w_in, seq, off, deps=prev)
                q, k, v, xc, gates = ml_qkv(zz, ml_conv_w[j], ml_conv_b[j], wq, wk, wv,
                                            ml_w_gates[j], ml_b_gates[j], gb, seq)
                a = ml_chunk(q, k, v, xc, zz, gates, gates.T, ml_norm_g[j], ml_skip[j], gb, seq)
            xt, h2 = mm_res_ln(a, w_out, b_out, xt, mod4, 2, 4, 3, ln1_g[i], ln1_b[i], seq, alpha, off)
            route = moe_route(h2, moe_w_router[i], moe_b_router[i])
            prev = (route["order_token"],)
            mods.append(mod4); routes.append(route); x_mid.append(xt); h_mid.append(h2)
        ysorted = []
        for g in range(groups):
            r = routes[g]
            rows = h_mid[g].at[r["src"]].get(mode=_IN_BOUNDS)
            ysorted.append(experts(rows, r["tile_e"], r["n_used"], r["tile_next"], moe_w_gate_up,
                                   moe_b_gate_up, moe_w_down, moe_b_down, i, deps=prev))
            prev = (ysorted[g],)
        for g in range(groups):
            r = routes[g]
            ys = [ysorted[g].at[r["dest4"][k]].get(mode=_IN_BOUNDS) for k in range(TOP_K)]
            xs[g] = combine_ln(ys, r["prob"], x_mid[g], mods[g], ln2_g[i], ln2_b[i], seq, alpha, deps=prev)
            prev = (xs[g],)
            offs[g] = 0
    return jnp.concatenate(xs, axis=0).reshape(bsz, seq, d)
```

```python
import functools
import math

import jax
import jax.numpy as jnp
from jax import lax
from jax.experimental import pallas as pl
from jax.experimental.pallas import tpu as pltpu

F32 = jnp.float32
BF16 = jnp.bfloat16

LN_EPS = 1e-5
RG_C = 8.0
CONV_W = 4
CHUNK = 64
SG_BLOCK = 128
SG_GROUPS = 8
ML_HEADS = 8
TOP_K = 4
SWIGLU_LIMIT = 7.0
SWIGLU_ALPHA = 1.702

SUBLANES = 8
RG_SUPER = 640
ML_L = 256
EXPERT_TILE = 512
BATCH_GROUPS = 2
VMEM_LIMIT = 56 * 1024 * 1024
NEG = -1e30
_IN_BOUNDS = "promise_in_bounds"


def _cparams(sem):
    return pltpu.CompilerParams(dimension_semantics=sem, vmem_limit_bytes=VMEM_LIMIT)


def _sigmoid(x):
    return 1.0 / (1.0 + jnp.exp(-x))


def _gelu_tanh(x):
    return 0.5 * x * (1.0 + jnp.tanh(math.sqrt(2.0 / math.pi) * (x + 0.044715 * (x * x * x))))


def _softplus(x):
    return jnp.maximum(x, 0.0) + jnp.log1p(jnp.exp(-jnp.abs(x)))


def _layer_norm_rows(v, g, b):
    mu = jnp.mean(v, axis=-1, keepdims=True)
    d = v - mu
    var = jnp.mean(d * d, axis=-1, keepdims=True)
    return d * lax.rsqrt(var + LN_EPS) * g + b


def _pack_rows(v):
    n = v.shape[1] // 2
    lo = lax.bitcast_convert_type(v[:, :n].astype(BF16).astype(F32), jnp.uint32)
    hi = lax.bitcast_convert_type(v[:, n:].astype(BF16).astype(F32), jnp.uint32)
    return (lo >> 16) | hi


def _unpack_rows(w):
    lo = lax.bitcast_convert_type(w << 16, F32)
    hi = lax.bitcast_convert_type(w & jnp.uint32(0xFFFF0000), F32)
    return jnp.concatenate([lo.astype(BF16), hi.astype(BF16)], axis=1)


def _const_spec(shape):
    nd = len(shape)
    return pl.BlockSpec(shape, lambda *_: (0,) * nd, pipeline_mode=pl.Buffered(1))


def _after(kernel_fn, n_in, deps):
    nd = len(deps)
    if nd == 0:
        return kernel_fn

    def body(*refs):
        return kernel_fn(*refs[:n_in], *refs[n_in + nd:])

    return body


def _dep_specs(deps):
    return [pl.BlockSpec(memory_space=pl.ANY)] * len(deps)


def _ada_kernel(c_ref, w_ref, b_ref, o_ref):
    c = c_ref[...]
    cond = c * _sigmoid(c)
    o_ref[...] = jnp.dot(cond.astype(BF16), w_ref[...].astype(BF16),
                         preferred_element_type=F32) + b_ref[...]


def ada_mod(c, ada_w, ada_b, tn=1024):
    depth, d, n = ada_w.shape
    bsz = c.shape[0]
    return pl.pallas_call(
        _ada_kernel,
        grid=(depth, n // tn),
        in_specs=[
            pl.BlockSpec((bsz, d), lambda l, j: (0, 0)),
            pl.BlockSpec((None, d, tn), lambda l, j: (l, 0, j)),
            pl.BlockSpec((None, 1, tn), lambda l, j: (l, 0, j)),
        ],
        out_specs=pl.BlockSpec((None, bsz, tn), lambda l, j: (l, 0, j)),
        out_shape=jax.ShapeDtypeStruct((depth, bsz, n), F32),
        compiler_params=_cparams(("parallel", "parallel")),
        name="ada_mod",
    )(c, ada_w, ada_b.reshape(depth, 1, n))


def _mm_in_kernel(x_ref, sc_ref, sh_ref, w_ref, o_ref, *, tn):
    h = (x_ref[...] * (1.0 + sc_ref[...]) + sh_ref[...]).astype(BF16)
    for j in range(o_ref.shape[1] // tn):
        cols = slice(j * tn, (j + 1) * tn)
        o_ref[:, cols] = jnp.dot(h, w_ref[:, cols], preferred_element_type=F32).astype(o_ref.dtype)


def mm_in(x, mod4, k_sc, k_sh, w, seq, row_off=0, deps=(), tm=512, tn=1024):
    t, d = mod4.shape[0] * seq, x.shape[1]
    n = w.shape[1]
    tm = min(tm, seq)
    rows_per_b = seq // tm
    blk_off = row_off // tm
    return pl.pallas_call(
        _after(functools.partial(_mm_in_kernel, tn=tn), 4, deps),
        grid=(t // tm,),
        in_specs=[
            pl.BlockSpec((tm, d), lambda i: (i + blk_off, 0)),
            pl.BlockSpec((None, None, 1, d), lambda i: (i // rows_per_b, k_sc, 0, 0)),
            pl.BlockSpec((None, None, 1, d), lambda i: (i // rows_per_b, k_sh, 0, 0)),
            _const_spec((d, n)),
        ] + _dep_specs(deps),
        out_specs=pl.BlockSpec((tm, n), lambda i: (i, 0)),
        out_shape=jax.ShapeDtypeStruct((t, n), BF16),
        compiler_params=_cparams(("parallel",)),
        name="mm_in",
    )(x, mod4, mod4, w, *deps)


def _mm_res_ln_kernel(a_ref, w_ref, b_ref, x_ref, g_ref, lng_ref, lnb_ref, sc_ref, sh_ref,
                      xo_ref, ho_ref, *, alpha):
    y = jnp.dot(a_ref[...], w_ref[...], preferred_element_type=F32) + b_ref[...]
    v = alpha * x_ref[...] + (1.0 + g_ref[...]) * y
    xn = _layer_norm_rows(v, lng_ref[...], lnb_ref[...])
    xo_ref[...] = xn
    ho_ref[...] = _pack_rows(xn * (1.0 + sc_ref[...]) + sh_ref[...])


def mm_res_ln(a, w, bias, x, mod4, k_gate, k_sc, k_sh, ln_g, ln_b, seq, alpha, row_off=0, tm=256):
    t, k = a.shape
    d = x.shape[1]
    rows_per_b = seq // tm
    blk_off = row_off // tm
    modspec = lambda kk: pl.BlockSpec((None, None, 1, d), lambda i: (i // rows_per_b, kk, 0, 0))
    return pl.pallas_call(
        functools.partial(_mm_res_ln_kernel, alpha=alpha),
        grid=(t // tm,),
        in_specs=[
            pl.BlockSpec((tm, k), lambda i: (i, 0)),
            _const_spec((k, d)),
            _const_spec((1, d)),
            pl.BlockSpec((tm, d), lambda i: (i + blk_off, 0)),
            modspec(k_gate),
            _const_spec((1, d)),
            _const_spec((1, d)),
            modspec(k_sc),
            modspec(k_sh),
        ],
        out_specs=[pl.BlockSpec((tm, d), lambda i: (i, 0)), pl.BlockSpec((tm, d // 2), lambda i: (i, 0))],
        out_shape=[jax.ShapeDtypeStruct((t, d), F32), jax.ShapeDtypeStruct((t, d // 2), jnp.uint32)],
        compiler_params=_cparams(("parallel",)),
        name="mm_res_ln",
    )(a, w, bias.reshape(1, d), x, mod4, ln_g.reshape(1, d), ln_b.reshape(1, d), mod4, mod4)


def _causal_conv(buf_ref, x, w_ref, b_ref, first):
    ts = x.shape[0]

    @pl.when(first)
    def _():
        buf_ref[0:SUBLANES, :] = jnp.zeros((SUBLANES, x.shape[1]), F32)

    buf_ref[SUBLANES:SUBLANES + ts, :] = x
    y = b_ref[...] + w_ref[CONV_W - 1:CONV_W, :] * x
    for back in range(1, CONV_W):
        y = y + w_ref[CONV_W - 1 - back:CONV_W - back, :] * buf_ref[SUBLANES - back:SUBLANES - back + ts, :]
    buf_ref[0:SUBLANES, :] = buf_ref[ts:ts + SUBLANES, :]
    return y


def _scan_rows(a, u, h0):
    ts, c = a.shape
    groups = ts // SUBLANES
    a3 = a.reshape(groups, SUBLANES, c)
    u3 = u.reshape(groups, SUBLANES, c)
    rid = lax.broadcasted_iota(jnp.int32, a3.shape, 1)
    for k in (1, 2, 4):
        keep = rid >= k
        a_sh = jnp.where(keep, pltpu.roll(a3, k, 1), 1.0)
        u_sh = jnp.where(keep, pltpu.roll(u3, k, 1), 0.0)
        u3 = u3 + a3 * u_sh
        a3 = a3 * a_sh
    h = h0
    outs = []
    for j in range(groups):
        hj = u3[j] + a3[j] * h
        outs.append(hj)
        h = hj[SUBLANES - 1:SUBLANES, :]
    return jnp.concatenate(outs, axis=0), h


def _rg_core_kernel(gate_ref, rec_ref, cw_ref, cb_ref, wri_ref, br_ref, bi_ref, lam_ref,
                    o_ref, buf_ref, h_ref):
    first = pl.program_id(2) == 0
    c = o_ref.shape[1]
    xr = _causal_conv(buf_ref, rec_ref[...].astype(F32), cw_ref, cb_ref, first)
    pre = jnp.dot(xr.astype(BF16), wri_ref[...], preferred_element_type=F32)
    r = _sigmoid(pre[:, :c] + br_ref[...])
    i = _sigmoid(pre[:, c:] + bi_ref[...])
    log_a = (-RG_C) * r * _softplus(-lam_ref[...])
    a = jnp.exp(log_a)
    u = jnp.sqrt(-jnp.tanh(log_a) * (a * a + 1.0)) * (i * xr)

    @pl.when(first)
    def _():
        h_ref[...] = jnp.zeros_like(h_ref)

    hs, h_last = _scan_rows(a, u, h_ref[...])
    h_ref[...] = h_last
    o_ref[...] = (_gelu_tanh(gate_ref[...].astype(F32)) * hs).astype(BF16)


def rg_core(z, conv_w, conv_b, w_ri, b_r, b_i, lam, bsz, seq, ts=256):
    t = z.shape[0]
    width = conv_w.shape[1]
    nsup = width // RG_SUPER
    spb = seq // ts
    row = lambda c, b, s: b * spb + s
    vec = lambda: pl.BlockSpec((1, RG_SUPER), lambda c, b, s: (0, c))
    return pl.pallas_call(
        _rg_core_kernel,
        grid=(nsup, bsz, spb),
        in_specs=[
            pl.BlockSpec((ts, RG_SUPER), lambda c, b, s: (row(c, b, s), c)),
            pl.BlockSpec((ts, RG_SUPER), lambda c, b, s: (row(c, b, s), nsup + c)),
            pl.BlockSpec((CONV_W, RG_SUPER), lambda c, b, s: (0, c)),
            vec(),
            pl.BlockSpec((None, RG_SUPER, 2 * RG_SUPER), lambda c, b, s: (c, 0, 0)),
            vec(), vec(), vec(),
        ],
        out_specs=pl.BlockSpec((ts, RG_SUPER), lambda c, b, s: (row(c, b, s), c)),
        out_shape=jax.ShapeDtypeStruct((t, width), BF16),
        scratch_shapes=[pltpu.VMEM((ts + SUBLANES, RG_SUPER), F32), pltpu.VMEM((1, RG_SUPER), F32)],
        compiler_params=_cparams(("parallel", "parallel", "arbitrary")),
        name="rg_core",
    )(z, z, conv_w, conv_b.reshape(1, width), w_ri, b_r.reshape(1, width), b_i.reshape(1, width),
      lam.reshape(1, width))


def _rg_gate_weights(w_r, w_i):
    nb, bs, _ = w_r.shape
    per = RG_SUPER // bs
    nsup = nb // per

    def dense(w):
        w4 = w.reshape(nsup, per, bs, bs)
        eye = jnp.eye(per, dtype=w.dtype)
        return jnp.einsum("spcd,pq->spcqd", w4, eye).reshape(nsup, RG_SUPER, RG_SUPER)

    return jnp.concatenate([dense(w_r), dense(w_i)], axis=-1).astype(BF16)


def _sg_in_kernel(x_ref, sc_ref, sh_ref, w_ref, b_ref, lng_ref, lnb_ref, u_ref, v_ref):
    half = u_ref.shape[1]
    h = (x_ref[...] * (1.0 + sc_ref[...]) + sh_ref[...]).astype(BF16)
    zu = jnp.dot(h, w_ref[:, :half], preferred_element_type=F32) + b_ref[:, :half]
    u_ref[...] = _gelu_tanh(zu).astype(BF16)
    zv = jnp.dot(h, w_ref[:, half:], preferred_element_type=F32) + b_ref[:, half:]
    v_ref[...] = _layer_norm_rows(_gelu_tanh(zv), lng_ref[...], lnb_ref[...]).astype(BF16)


def sg_in(x, mod4, w, b, ln_g, ln_b, seq, deps=(), tm=256):
    t, d = x.shape
    half = w.shape[1] // 2
    rows_per_b = seq // tm
    modspec = lambda kk: pl.BlockSpec((None, None, 1, d), lambda i: (i // rows_per_b, kk, 0, 0))
    row = lambda i: (i, 0)
    return pl.pallas_call(
        _after(_sg_in_kernel, 7, deps),
        grid=(t // tm,),
        in_specs=[
            pl.BlockSpec((tm, d), row),
            modspec(1), modspec(0),
            _const_spec((d, 2 * half)),
            _const_spec((1, 2 * half)),
            _const_spec((1, half)), _const_spec((1, half)),
        ] + _dep_specs(deps),
        out_specs=[pl.BlockSpec((tm, half), row), pl.BlockSpec((tm, half), row)],
        out_shape=[jax.ShapeDtypeStruct((t, half), BF16), jax.ShapeDtypeStruct((t, half), BF16)],
        compiler_params=_cparams(("parallel",)),
        name="sg_in",
    )(x, mod4, mod4, w, b.reshape(1, 2 * half), ln_g.reshape(1, half), ln_b.reshape(1, half), *deps)


def _sg_mix_kernel(u_ref, v_ref, w_ref, bt_ref, o_ref):
    nblk = u_ref.shape[0] // SG_BLOCK
    gw = u_ref.shape[1] // SG_GROUPS
    ii = lax.broadcasted_iota(jnp.int32, (SG_BLOCK, SG_BLOCK), 0)
    jj = lax.broadcasted_iota(jnp.int32, (SG_BLOCK, SG_BLOCK), 1)
    allowed = (jj // CHUNK) <= (ii // CHUNK)
    for g in range(SG_GROUPS):
        w = jnp.where(allowed, w_ref[g], 0.0).astype(BF16)
        bias = bt_ref[:, g:g + 1]
        for n in range(nblk):
            rows = slice(n * SG_BLOCK, (n + 1) * SG_BLOCK)
            cols = slice(g * gw, (g + 1) * gw)
            mixed = jnp.dot(w, v_ref[rows, cols], preferred_element_type=F32) + bias
            o_ref[rows, cols] = (u_ref[rows, cols].astype(F32) * mixed).astype(BF16)


def sg_mix(u, v, w_sp, b_sp, tm=256):
    t, half = u.shape
    return pl.pallas_call(
        _sg_mix_kernel,
        grid=(t // tm,),
        in_specs=[
            pl.BlockSpec((tm, half), lambda i: (i, 0)),
            pl.BlockSpec((tm, half), lambda i: (i, 0)),
            _const_spec(w_sp.shape),
            _const_spec((SG_BLOCK, SG_GROUPS)),
        ],
        out_specs=pl.BlockSpec((tm, half), lambda i: (i, 0)),
        out_shape=jax.ShapeDtypeStruct((t, half), BF16),
        compiler_params=_cparams(("parallel",)),
        name="sg_mix",
    )(u, v, w_sp, b_sp.T)


def _ml_qkv_kernel(xm_ref, cw_ref, cb_ref, wq_ref, wk_ref, wv_ref, wg_ref, bg_ref,
                   q_ref, k_ref, v_ref, xc_ref, g_ref, buf_ref):
    first = pl.program_id(1) == 0
    width = xm_ref.shape[1]
    hd = width // ML_HEADS
    xm = xm_ref[...]
    conv = _causal_conv(buf_ref, xm.astype(F32), cw_ref, cb_ref, first)
    xc = (conv * _sigmoid(conv)).astype(BF16)
    xc_ref[...] = xc
    for h in range(ML_HEADS):
        cols = slice(h * hd, (h + 1) * hd)
        q_ref[:, cols] = jnp.dot(xc[:, cols], wq_ref[h], preferred_element_type=F32).astype(BF16)
        k_ref[:, cols] = jnp.dot(xc[:, cols], wk_ref[h], preferred_element_type=F32).astype(BF16)
        v_ref[:, cols] = jnp.dot(xm[:, cols], wv_ref[h], preferred_element_type=F32).astype(BF16)
    gates = (jnp.dot(q_ref[...], wg_ref[0], preferred_element_type=F32)
             + jnp.dot(k_ref[...], wg_ref[1], preferred_element_type=F32)
             + jnp.dot(v_ref[...], wg_ref[2], preferred_element_type=F32) + bg_ref[...])
    is_f = lax.broadcasted_iota(jnp.int32, gates.shape, 1) >= ML_HEADS
    g_ref[...] = jnp.where(is_f, -_softplus(-gates), gates)


def _ml_headwise_weights(w):
    nqb, bs, _ = w.shape
    per = nqb // ML_HEADS
    w4 = w.reshape(ML_HEADS, per, bs, bs)
    eye = jnp.eye(per, dtype=w.dtype)
    return jnp.einsum("hpcd,pq->hpcqd", w4, eye).reshape(ML_HEADS, per * bs, per * bs).astype(BF16)


def ml_qkv(zz, conv_w, conv_b, wq, wk, wv, w_gates, b_gates, bsz, seq, ts=256):
    t = zz.shape[0]
    width = conv_w.shape[1]
    hd = width // ML_HEADS
    spb = seq // ts
    row = lambda b, s: (b * spb + s, 0)
    act = lambda: pl.BlockSpec((ts, width), row)
    return pl.pallas_call(
        _ml_qkv_kernel,
        grid=(bsz, spb),
        in_specs=[
            act(),
            _const_spec((CONV_W, width)), _const_spec((1, width)),
            _const_spec((ML_HEADS, hd, hd)), _const_spec((ML_HEADS, hd, hd)), _const_spec((ML_HEADS, hd, hd)),
            _const_spec((3, width, 2 * ML_HEADS)), _const_spec((1, 2 * ML_HEADS)),
        ],
        out_specs=[act(), act(), act(), act(), pl.BlockSpec((ts, 2 * ML_HEADS), row)],
        out_shape=[jax.ShapeDtypeStruct((t, width), BF16)] * 4
                  + [jax.ShapeDtypeStruct((t, 2 * ML_HEADS), F32)],
        scratch_shapes=[pltpu.VMEM((ts + SUBLANES, width), F32)],
        compiler_params=_cparams(("parallel", "arbitrary")),
        name="ml_qkv",
    )(zz, conv_w, conv_b.reshape(1, width), wq, wk, wv,
      w_gates.reshape(3, width, 2 * ML_HEADS).astype(BF16), b_gates.reshape(1, 2 * ML_HEADS))


def _ml_chunk_kernel(q_ref, k_ref, v_ref, xc_ref, z_ref, gc_ref, gr_ref, ng_ref, skip_ref,
                     o_ref, c_scr, n_scr, m_scr):
    L, width = q_ref.shape
    hd = width // ML_HEADS
    scale = hd ** -0.5

    @pl.when(pl.program_id(1) == 0)
    def _():
        c_scr[...] = jnp.zeros_like(c_scr)
        n_scr[...] = jnp.zeros_like(n_scr)
        m_scr[...] = jnp.zeros_like(m_scr)

    ri = lax.broadcasted_iota(jnp.int32, (L, L), 0)
    ci = lax.broadcasted_iota(jnp.int32, (L, L), 1)
    causal = ri >= ci
    lower = causal.astype(F32)
    upper = (ri <= ci).astype(F32)
    gc = gc_ref[...]
    gr = gr_ref[...]
    bcol = jnp.dot(lower, gc[:, ML_HEADS:], preferred_element_type=F32, precision=lax.Precision.HIGHEST)
    brow = jnp.dot(gr[ML_HEADS:, :], upper, preferred_element_type=F32, precision=lax.Precision.HIGHEST)

    for h in range(ML_HEADS):
        cols = slice(h * hd, (h + 1) * hd)
        bc = bcol[:, h:h + 1]
        br = brow[h:h + 1, :]
        lir = gr[h:h + 1, :]
        lic = gc[:, h:h + 1]
        m_prev = m_scr[h:h + 1, 0:1]
        g = bc + m_prev
        dmat = jnp.where(causal, bc - br + lir, NEG)
        m_t = jnp.maximum(g, jnp.max(dmat, axis=-1, keepdims=True))
        w_intra = jnp.exp(dmat - m_t)
        w_inter = jnp.exp(g - m_t)
        qh = q_ref[:, cols]
        kh = k_ref[:, cols]
        vh = v_ref[:, cols]
        s = lax.dot_general(qh, kh, (((1,), (1,)), ((), ())), preferred_element_type=F32) * scale * w_intra
        c_old = c_scr[h]
        n_old = n_scr[h:h + 1, :]
        num = (jnp.dot(s.astype(BF16), vh, preferred_element_type=F32)
               + w_inter * jnp.dot(qh, c_old.astype(BF16), preferred_element_type=F32))
        den = (jnp.sum(s, axis=-1, keepdims=True)
               + w_inter * jnp.sum(qh.astype(F32) * n_old, axis=-1, keepdims=True))
        out = num / jnp.maximum(jnp.abs(den), jnp.exp(-m_t))

        b_last = bc[L - 1:L, :]
        dec = b_last - bc + lic
        m_new = jnp.maximum(b_last + m_prev, jnp.max(dec, axis=0, keepdims=True))
        ws = jnp.exp(dec - m_new)
        wc = jnp.exp(b_last + m_prev - m_new)
        kw = kh.astype(F32) * (ws * scale)
        c_scr[h] = wc * c_old + lax.dot_general(kw.astype(BF16), vh, (((0,), (0,)), ((), ())),
                                                preferred_element_type=F32)
        n_scr[h:h + 1, :] = wc * n_old + jnp.sum(kw, axis=0, keepdims=True)
        m_scr[h:h + 1, :] = jnp.broadcast_to(m_new, (1, m_scr.shape[1]))

        mu = jnp.mean(out, axis=-1, keepdims=True)
        dlt = out - mu
        var = jnp.mean(dlt * dlt, axis=-1, keepdims=True)
        hn = dlt * lax.rsqrt(var + LN_EPS) * ng_ref[:, cols]
        hn = hn + skip_ref[:, cols] * xc_ref[:, cols].astype(F32)
        o_ref[:, cols] = (_sigmoid(z_ref[:, cols].astype(F32)) * hn).astype(BF16)


def ml_chunk(q, k, v, xc, zz, gates, gates_t, norm_g, skip, bsz, seq):
    t, width = q.shape
    hd = width // ML_HEADS
    L = ML_L
    cpb = seq // L
    row = lambda b, c: (b * cpb + c, 0)
    act = lambda: pl.BlockSpec((L, width), row)
    return pl.pallas_call(
        _ml_chunk_kernel,
        grid=(bsz, cpb),
        in_specs=[
            act(), act(), act(), act(),
            pl.BlockSpec((L, width), lambda b, c: (b * cpb + c, 1)),
            pl.BlockSpec((L, 2 * ML_HEADS), row),
            pl.BlockSpec((2 * ML_HEADS, L), lambda b, c: (0, b * cpb + c)),
            _const_spec((1, width)), _const_spec((1, width)),
        ],
        out_specs=act(),
        out_shape=jax.ShapeDtypeStruct((t, width), BF16),
        scratch_shapes=[pltpu.VMEM((ML_HEADS, hd, hd), F32), pltpu.VMEM((ML_HEADS, hd), F32),
                        pltpu.VMEM((ML_HEADS, 128), F32)],
        compiler_params=_cparams(("parallel", "arbitrary")),
        name="ml_chunk",
    )(q, k, v, xc, zz, gates, gates_t, norm_g.reshape(1, width), skip.reshape(1, width))


def _router_kernel(h_ref, w_ref, b_ref, eid_ref, prob_ref, rank_ref, cnt_ref, cnt_scr):
    tr = h_ref.shape[0]
    ne = w_ref.shape[0]

    @pl.when(pl.program_id(0) == 0)
    def _():
        cnt_scr[...] = jnp.zeros_like(cnt_scr)

    logits = lax.dot_general(w_ref[...], _unpack_rows(h_ref[...]), (((1,), (1,)), ((), ())),
                             preferred_element_type=F32) + b_ref[...]
    sub = lax.broadcasted_iota(jnp.int32, (ne, tr), 0)
    work = logits
    sel = jnp.zeros((ne, tr), F32)
    hits = []
    top = None
    for k in range(TOP_K):
        m = jnp.max(work, axis=0, keepdims=True)
        idx = jnp.min(jnp.where(work == m, sub, ne), axis=0, keepdims=True)
        hit = sub == idx
        hits.append((hit, idx))
        sel = jnp.where(hit, 1.0, sel)
        work = jnp.where(hit, -jnp.inf, work)
        if k == 0:
            top = m
    e = sel * jnp.exp(logits - top)
    p = e / jnp.sum(e, axis=0, keepdims=True)

    ri = lax.broadcasted_iota(jnp.int32, (tr, tr), 0)
    ci = lax.broadcasted_iota(jnp.int32, (tr, tr), 1)
    strict_upper = (ri < ci).astype(BF16)
    before = jnp.dot(sel.astype(BF16), strict_upper, preferred_element_type=F32)
    rank = cnt_scr[...] + before
    cnt_scr[...] = cnt_scr[...] + jnp.sum(sel, axis=1, keepdims=True)
    cnt_ref[...] = cnt_scr[...].astype(jnp.int32)

    row = lax.broadcasted_iota(jnp.int32, (SUBLANES, tr), 0)
    eid = jnp.zeros((SUBLANES, tr), jnp.int32)
    prob = jnp.zeros((SUBLANES, tr), F32)
    rnk = jnp.zeros((SUBLANES, tr), F32)
    for k, (hit, idx) in enumerate(hits):
        slot = row == k
        eid = jnp.where(slot, idx, eid)
        prob = jnp.where(slot, jnp.sum(jnp.where(hit, p, 0.0), axis=0, keepdims=True), prob)
        rnk = jnp.where(slot, jnp.sum(jnp.where(hit, rank, 0.0), axis=0, keepdims=True), rnk)
    eid_ref[...] = eid
    prob_ref[...] = prob
    rank_ref[...] = rnk.astype(jnp.int32)


def router(h2, w_router, b_router, tr=512):
    t, dp = h2.shape
    d, ne = w_router.shape
    tr = min(tr, t)
    col = lambda i: (0, i)
    return pl.pallas_call(
        _router_kernel,
        grid=(t // tr,),
        in_specs=[pl.BlockSpec((tr, dp), lambda i: (i, 0)), _const_spec((ne, d)), _const_spec((ne, 1))],
        out_specs=[pl.BlockSpec((SUBLANES, tr), col), pl.BlockSpec((SUBLANES, tr), col),
                   pl.BlockSpec((SUBLANES, tr), col), pl.BlockSpec((ne, 1), lambda i: (0, 0))],
        out_shape=[jax.ShapeDtypeStruct((SUBLANES, t), jnp.int32), jax.ShapeDtypeStruct((SUBLANES, t), F32),
                   jax.ShapeDtypeStruct((SUBLANES, t), jnp.int32), jax.ShapeDtypeStruct((ne, 1), jnp.int32)],
        scratch_shapes=[pltpu.VMEM((ne, 1), F32)],
        compiler_params=_cparams(("arbitrary",)),
        name="router",
    )(h2, w_router.T.astype(BF16), b_router.reshape(ne, 1))


def _expert_kernel(te_ref, nu_ref, nx_ref, x_ref, wgu_hbm, bgu_ref, wd_hbm, bd_ref, o_ref,
                   wgu_f32, wd_f32, wgu_bf, wd_bf, sem, *, layer):
    i = pl.program_id(0)
    used = i < nu_ref[0]
    e = te_ref[i]
    new_expert = jnp.logical_or(i == 0, e != te_ref[jnp.maximum(i - 1, 0)])

    def fetch(expert):
        return (pltpu.make_async_copy(wgu_hbm.at[layer, expert], wgu_f32, sem.at[0]),
                pltpu.make_async_copy(wd_hbm.at[layer, expert], wd_f32, sem.at[1]))

    @pl.when(jnp.logical_and(used, i == 0))
    def _():
        for cp in fetch(e):
            cp.start()

    @pl.when(jnp.logical_and(used, new_expert))
    def _():
        for cp in fetch(e):
            cp.wait()
        wgu_bf[...] = wgu_f32[...].astype(BF16)
        wd_bf[...] = wd_f32[...].astype(BF16)
        nxt = nx_ref[i]

        @pl.when(nxt != e)
        def _():
            for cp in fetch(nxt):
                cp.start()

    @pl.when(used)
    def _():
        f = wd_bf.shape[0]
        gu = jnp.dot(_unpack_rows(x_ref[...]), wgu_bf[...], preferred_element_type=F32) + bgu_ref[...]
        gate = jnp.minimum(gu[:, :f], SWIGLU_LIMIT)
        up = jnp.clip(gu[:, f:], -SWIGLU_LIMIT, SWIGLU_LIMIT)
        act = gate * _sigmoid(SWIGLU_ALPHA * gate) * (up + 1.0)
        y = jnp.dot(act.astype(BF16), wd_bf[...], preferred_element_type=F32) + bd_ref[...]
        o_ref[...] = _pack_rows(y)

    @pl.when(jnp.logical_not(used))
    def _():
        o_ref[...] = jnp.zeros_like(o_ref)


def experts(xs, tile_e, n_used, tile_next, w_gu, b_gu, w_d, b_d, layer, deps=()):
    p, dp = xs.shape
    depth, ne, d, f2 = w_gu.shape
    f = f2 // 2
    te = EXPERT_TILE
    grid_spec = pltpu.PrefetchScalarGridSpec(
        num_scalar_prefetch=3,
        grid=(p // te,),
        in_specs=[
            pl.BlockSpec((te, dp), lambda i, e, n, nx: (i, 0)),
            pl.BlockSpec(memory_space=pl.ANY),
            pl.BlockSpec((None, None, 1, f2), lambda i, e, n, nx: (layer, e[i], 0, 0)),
            pl.BlockSpec(memory_space=pl.ANY),
            pl.BlockSpec((None, None, 1, d), lambda i, e, n, nx: (layer, e[i], 0, 0)),
        ] + _dep_specs(deps),
        out_specs=pl.BlockSpec((te, dp), lambda i, e, n, nx: (i, 0)),
        scratch_shapes=[pltpu.VMEM((d, f2), F32), pltpu.VMEM((f, d), F32),
                        pltpu.VMEM((d, f2), BF16), pltpu.VMEM((f, d), BF16),
                        pltpu.SemaphoreType.DMA((2,))],
    )
    return pl.pallas_call(
        _after(functools.partial(_expert_kernel, layer=layer), 8, deps),
        grid_spec=grid_spec,
        out_shape=jax.ShapeDtypeStruct((p, dp), jnp.uint32),
        compiler_params=_cparams(("arbitrary",)),
        name="experts",
    )(tile_e, n_used, tile_next, xs, w_gu, b_gu.reshape(depth, ne, 1, f2), w_d,
      b_d.reshape(depth, ne, 1, d), *deps)


def _combine_ln_kernel(y0_ref, y1_ref, y2_ref, y3_ref, p_ref, x_ref, g_ref, lng_ref, lnb_ref, xo_ref,
                       *, alpha):
    p = p_ref[...]
    y = None
    for k, y_ref in enumerate((y0_ref, y1_ref, y2_ref, y3_ref)):
        term = p[:, k:k + 1] * _unpack_rows(y_ref[...]).astype(F32)
        y = term if y is None else y + term
    v = alpha * x_ref[...] + (1.0 + g_ref[...]) * y
    xo_ref[...] = _layer_norm_rows(v, lng_ref[...], lnb_ref[...])


def combine_ln(ys, prob, x, mod4, ln_g, ln_b, seq, alpha, deps=(), tm=256):
    t, d = x.shape
    ne = prob.shape[1]
    rows_per_b = seq // tm
    row = lambda i: (i, 0)
    return pl.pallas_call(
        _after(functools.partial(_combine_ln_kernel, alpha=alpha), 9, deps),
        grid=(t // tm,),
        in_specs=[pl.BlockSpec((tm, d // 2), row)] * 4 + [
            pl.BlockSpec((tm, ne), row),
            pl.BlockSpec((tm, d), row),
            pl.BlockSpec((None, None, 1, d), lambda i: (i // rows_per_b, 5, 0, 0)),
            _const_spec((1, d)), _const_spec((1, d)),
        ] + _dep_specs(deps),
        out_specs=pl.BlockSpec((tm, d), row),
        out_shape=jax.ShapeDtypeStruct((t, d), F32),
        compiler_params=_cparams(("parallel",)),
        name="combine_ln",
    )(*ys, prob, x, mod4, ln_g.reshape(1, d), ln_b.reshape(1, d), *deps)


def moe_route(h2, w_router, b_router):
    t, d = h2.shape
    ne = w_router.shape[1]
    te = EXPERT_TILE
    eid, prob, rank, counts2d = router(h2, w_router, b_router)
    counts = counts2d[:, 0]
    padded = ((counts + te - 1) // te) * te
    ends = jnp.cumsum(padded)
    starts = ends - padded
    prob = prob.T
    eid4 = eid[:TOP_K]
    first_slot = jnp.zeros_like(eid4)
    for e in range(ne):
        first_slot = jnp.where(eid4 == e, starts[e], first_slot)
    dest4 = first_slot + rank[:TOP_K]
    n_slots = t * TOP_K + ne * te
    tok = jnp.broadcast_to(jnp.arange(t, dtype=jnp.int32)[None, :], (TOP_K, t))
    src = jnp.zeros((n_slots,), jnp.int32).at[dest4.reshape(-1)].set(
        tok.reshape(-1), unique_indices=True, mode=_IN_BOUNDS)
    n_tiles = n_slots // te
    tile_start = jnp.arange(n_tiles, dtype=jnp.int32) * te
    tile_e = jnp.sum((ends[None, :] <= tile_start[:, None]).astype(jnp.int32), axis=1)
    n_used = (ends[-1] // te).astype(jnp.int32).reshape(1)
    last_e = jnp.sum((ends <= ends[-1] - 1).astype(jnp.int32))
    tile_e = jnp.where(tile_start < ends[-1], tile_e, last_e).astype(jnp.int32)
    ids = jnp.arange(ne, dtype=jnp.int32)
    later = jnp.where((ids[None, :] > ids[:, None]) & (padded[None, :] > 0), ids[None, :], ne)
    nxt = jnp.min(later, axis=1)
    nxt = jnp.where(nxt == ne, ids, nxt)
    tile_next = jnp.sum(jnp.where(tile_e[:, None] == ids[None, :], nxt[None, :], 0), axis=1).astype(jnp.int32)
    return dict(src=src, dest4=dest4, prob=prob, tile_e=tile_e, n_used=n_used, tile_next=tile_next,
                order_token=counts2d)


def kernel(x, c, ada_w, ada_b, ln1_g, ln1_b, ln2_g, ln2_b, rg_w_in, rg_conv_w, rg_conv_b, rg_w_rgate, rg_b_rgate, rg_w_igate, rg_b_igate, rg_lam, rg_w_out, sg_w_in, sg_b_in, sg_ln_g, sg_ln_b, sg_w_sp, sg_b_sp, sg_w_out, sg_b_out, ml_w_in, ml_conv_w, ml_conv_b, ml_w_q, ml_w_k, ml_w_v, ml_w_gates, ml_b_gates, ml_skip, ml_norm_g, ml_w_out, moe_w_router, moe_b_router, moe_w_gate_up, moe_b_gate_up, moe_w_down, moe_b_down):
    bsz, seq, d = x.shape
    depth = ada_w.shape[0]
    alpha = (2 * depth) ** 0.25
    xfull = x.reshape(bsz * seq, d)
    mod = ada_mod(c, ada_w, ada_b)
    zero_bias = jnp.zeros((d,), F32)

    groups = BATCH_GROUPS if bsz % BATCH_GROUPS == 0 else 1
    gb = bsz // groups
    xs = [xfull] * groups
    offs = [g * gb * seq for g in range(groups)]

    for i in range(depth):
        kind, j = i % 3, i // 3
        if kind == 0:
            w_in = rg_w_in[j].astype(BF16)
            w_ri = _rg_gate_weights(rg_w_rgate[j], rg_w_igate[j])
            w_out, b_out = rg_w_out[j].astype(BF16), zero_bias
        elif kind == 1:
            w_in = sg_w_in[j].astype(BF16)
            w_out, b_out = sg_w_out[j].astype(BF16), sg_b_out[j]
        else:
            w_in = ml_w_in[j].astype(BF16)
            wq, wk, wv = (_ml_headwise_weights(w[j]) for w in (ml_w_q, ml_w_k, ml_w_v))
            w_out, b_out = ml_w_out[j].astype(BF16), zero_bias
        mods, routes, x_mid, h_mid = [], [], [], []
        prev = ()
        for g in range(groups):
            mod4 = mod[i, g * gb:(g + 1) * gb].reshape(gb, 6, 1, d)
            xt, off = xs[g], offs[g]
            if kind == 0:
                z = mm_in(xt, mod4, 1, 0, w_in, seq, off, deps=prev)
                a = rg_core(z, rg_conv_w[j], rg_conv_b[j], w_ri, rg_b_rgate[j], rg_b_igate[j], rg_lam[j],
                            gb, seq)
            elif kind == 1:
                u, v = sg_in(xt, mod4, w_in, sg_b_in[j], sg_ln_g[j], sg_ln_b[j], seq, deps=prev)
                a = sg_mix(u, v, sg_w_sp[j], sg_b_sp[j])
            else:
                zz = mm_in(xt, mod4, 1, 0, w_in, seq, off, deps=prev)
                q, k, v, xc, gates = ml_qkv(zz, ml_conv_w[j], ml_conv_b[j], wq, wk, wv,
                                            ml_w_gates[j], ml_b_gates[j], gb, seq)
                a = ml_chunk(q, k, v, xc, zz, gates, gates.T, ml_norm_g[j], ml_skip[j], gb, seq)
            xt, h2 = mm_res_ln(a, w_out, b_out, xt, mod4, 2, 4, 3, ln1_g[i], ln1_b[i], seq, alpha, off)
            route = moe_route(h2, moe_w_router[i], moe_b_router[i])
            prev = (route["order_token"],)
            mods.append(mod4); routes.append(route); x_mid.append(xt); h_mid.append(h2)
        ysorted = []
        for g in range(groups):
            r = routes[g]
            rows = h_mid[g].at[r["src"]].get(mode=_IN_BOUNDS)
            ysorted.append(experts(rows, r["tile_e"], r["n_used"], r["tile_next"], moe_w_gate_up,
                                   moe_b_gate_up, moe_w_down, moe_b_down, i, deps=prev))
            prev = (ysorted[g],)
        for g in range(groups):
            r = routes[g]
            ys = [ysorted[g].at[r["dest4"][k]].get(mode=_IN_BOUNDS) for k in range(TOP_K)]
            xs[g] = combine_ln(ys, r["prob"], x_mid[g], mods[g], ln2_g[i], ln2_b[i], seq, alpha, deps=prev)
            prev = (xs[g],)
            offs[g] = 0
    return jnp.concatenate(xs, axis=0).reshape(bsz, seq, d)
```

```python
import functools
import math

import jax
import jax.numpy as jnp
from jax import lax
from jax.experimental import pallas as pl
from jax.experimental.pallas import tpu as pltpu

F32 = jnp.float32
BF16 = jnp.bfloat16

LN_EPS = 1e-5
RG_C = 8.0
CONV_W = 4
CHUNK = 64
SG_BLOCK = 128
SG_GROUPS = 8
ML_HEADS = 8
TOP_K = 4
SWIGLU_LIMIT = 7.0
SWIGLU_ALPHA = 1.702

SUBLANES = 8
RG_SUPER = 640
ML_L = 256
EXPERT_TILE = 512
BATCH_GROUPS = 1
VMEM_LIMIT = 56 * 1024 * 1024
NEG = -1e30
_IN_BOUNDS = "promise_in_bounds"


def _cparams(sem):
    return pltpu.CompilerParams(dimension_semantics=sem, vmem_limit_bytes=VMEM_LIMIT)


def _sigmoid(x):
    return 1.0 / (1.0 + jnp.exp(-x))


def _gelu_tanh(x):
    return 0.5 * x * (1.0 + jnp.tanh(math.sqrt(2.0 / math.pi) * (x + 0.044715 * (x * x * x))))


def _softplus(x):
    return jnp.maximum(x, 0.0) + jnp.log1p(jnp.exp(-jnp.abs(x)))


def _layer_norm_rows(v, g, b):
    mu = jnp.mean(v, axis=-1, keepdims=True)
    d = v - mu
    var = jnp.mean(d * d, axis=-1, keepdims=True)
    return d * lax.rsqrt(var + LN_EPS) * g + b


def _pack_rows(v):
    n = v.shape[1] // 2
    lo = lax.bitcast_convert_type(v[:, :n].astype(BF16).astype(F32), jnp.uint32)
    hi = lax.bitcast_convert_type(v[:, n:].astype(BF16).astype(F32), jnp.uint32)
    return (lo >> 16) | hi


def _unpack_rows(w):
    lo = lax.bitcast_convert_type(w << 16, F32)
    hi = lax.bitcast_convert_type(w & jnp.uint32(0xFFFF0000), F32)
    return jnp.concatenate([lo.astype(BF16), hi.astype(BF16)], axis=1)


def _const_spec(shape):
    nd = len(shape)
    return pl.BlockSpec(shape, lambda *_: (0,) * nd, pipeline_mode=pl.Buffered(1))


def _after(kernel_fn, n_in, deps):
    nd = len(deps)
    if nd == 0:
        return kernel_fn

    def body(*refs):
        return kernel_fn(*refs[:n_in], *refs[n_in + nd:])

    return body


def _dep_specs(deps):
    return [pl.BlockSpec(memory_space=pl.ANY)] * len(deps)


def _ada_kernel(c_ref, w_ref, b_ref, o_ref):
    c = c_ref[...]
    cond = c * _sigmoid(c)
    o_ref[...] = jnp.dot(cond.astype(BF16), w_ref[...].astype(BF16),
                         preferred_element_type=F32) + b_ref[...]


def ada_mod(c, ada_w, ada_b, tn=1024):
    depth, d, n = ada_w.shape
    bsz = c.shape[0]
    return pl.pallas_call(
        _ada_kernel,
        grid=(depth, n // tn),
        in_specs=[
            pl.BlockSpec((bsz, d), lambda l, j: (0, 0)),
            pl.BlockSpec((None, d, tn), lambda l, j: (l, 0, j)),
            pl.BlockSpec((None, 1, tn), lambda l, j: (l, 0, j)),
        ],
        out_specs=pl.BlockSpec((None, bsz, tn), lambda l, j: (l, 0, j)),
        out_shape=jax.ShapeDtypeStruct((depth, bsz, n), F32),
        compiler_params=_cparams(("parallel", "parallel")),
        name="ada_mod",
    )(c, ada_w, ada_b.reshape(depth, 1, n))


def _mm_in_kernel(x_ref, sc_ref, sh_ref, w_ref, o_ref, *, tn):
    h = (x_ref[...] * (1.0 + sc_ref[...]) + sh_ref[...]).astype(BF16)
    for j in range(o_ref.shape[1] // tn):
        cols = slice(j * tn, (j + 1) * tn)
        o_ref[:, cols] = jnp.dot(h, w_ref[:, cols], preferred_element_type=F32).astype(o_ref.dtype)


def mm_in(x, mod4, k_sc, k_sh, w, seq, row_off=0, deps=(), tm=512, tn=1024):
    t, d = mod4.shape[0] * seq, x.shape[1]
    n = w.shape[1]
    tm = min(tm, seq)
    rows_per_b = seq // tm
    blk_off = row_off // tm
    return pl.pallas_call(
        _after(functools.partial(_mm_in_kernel, tn=tn), 4, deps),
        grid=(t // tm,),
        in_specs=[
            pl.BlockSpec((tm, d), lambda i: (i + blk_off, 0)),
            pl.BlockSpec((None, None, 1, d), lambda i: (i // rows_per_b, k_sc, 0, 0)),
            pl.BlockSpec((None, None, 1, d), lambda i: (i // rows_per_b, k_sh, 0, 0)),
            _const_spec((d, n)),
        ] + _dep_specs(deps),
        out_specs=pl.BlockSpec((tm, n), lambda i: (i, 0)),
        out_shape=jax.ShapeDtypeStruct((t, n), BF16),
        compiler_params=_cparams(("parallel",)),
        name="mm_in",
    )(x, mod4, mod4, w, *deps)


def _mm_res_ln_kernel(a_ref, w_ref, b_ref, x_ref, g_ref, lng_ref, lnb_ref, sc_ref, sh_ref,
                      xo_ref, ho_ref, *, alpha):
    y = jnp.dot(a_ref[...], w_ref[...], preferred_element_type=F32) + b_ref[...]
    v = alpha * x_ref[...] + (1.0 + g_ref[...]) * y
    xn = _layer_norm_rows(v, lng_ref[...], lnb_ref[...])
    xo_ref[...] = xn
    ho_ref[...] = _pack_rows(xn * (1.0 + sc_ref[...]) + sh_ref[...])


def mm_res_ln(a, w, bias, x, mod4, k_gate, k_sc, k_sh, ln_g, ln_b, seq, alpha, row_off=0, tm=256):
    t, k = a.shape
    d = x.shape[1]
    rows_per_b = seq // tm
    blk_off = row_off // tm
    modspec = lambda kk: pl.BlockSpec((None, None, 1, d), lambda i: (i // rows_per_b, kk, 0, 0))
    return pl.pallas_call(
        functools.partial(_mm_res_ln_kernel, alpha=alpha),
        grid=(t // tm,),
        in_specs=[
            pl.BlockSpec((tm, k), lambda i: (i, 0)),
            _const_spec((k, d)),
            _const_spec((1, d)),
            pl.BlockSpec((tm, d), lambda i: (i + blk_off, 0)),
            modspec(k_gate),
            _const_spec((1, d)),
            _const_spec((1, d)),
            modspec(k_sc),
            modspec(k_sh),
        ],
        out_specs=[pl.BlockSpec((tm, d), lambda i: (i, 0)), pl.BlockSpec((tm, d // 2), lambda i: (i, 0))],
        out_shape=[jax.ShapeDtypeStruct((t, d), F32), jax.ShapeDtypeStruct((t, d // 2), jnp.uint32)],
        compiler_params=_cparams(("parallel",)),
        name="mm_res_ln",
    )(a, w, bias.reshape(1, d), x, mod4, ln_g.reshape(1, d), ln_b.reshape(1, d), mod4, mod4)


def _causal_conv(buf_ref, x, w_ref, b_ref, first):
    ts = x.shape[0]

    @pl.when(first)
    def _():
        buf_ref[0:SUBLANES, :] = jnp.zeros((SUBLANES, x.shape[1]), F32)

    buf_ref[SUBLANES:SUBLANES + ts, :] = x
    y = b_ref[...] + w_ref[CONV_W - 1:CONV_W, :] * x
    for back in range(1, CONV_W):
        y = y + w_ref[CONV_W - 1 - back:CONV_W - back, :] * buf_ref[SUBLANES - back:SUBLANES - back + ts, :]
    buf_ref[0:SUBLANES, :] = buf_ref[ts:ts + SUBLANES, :]
    return y


def _scan_rows(a, u, h0):
    ts, c = a.shape
    groups = ts // SUBLANES
    a3 = a.reshape(groups, SUBLANES, c)
    u3 = u.reshape(groups, SUBLANES, c)
    rid = lax.broadcasted_iota(jnp.int32, a3.shape, 1)
    for k in (1, 2, 4):
        keep = rid >= k
        a_sh = jnp.where(keep, pltpu.roll(a3, k, 1), 1.0)
        u_sh = jnp.where(keep, pltpu.roll(u3, k, 1), 0.0)
        u3 = u3 + a3 * u_sh
        a3 = a3 * a_sh
    h = h0
    outs = []
    for j in range(groups):
        hj = u3[j] + a3[j] * h
        outs.append(hj)
        h = hj[SUBLANES - 1:SUBLANES, :]
    return jnp.concatenate(outs, axis=0), h


def _rg_core_kernel(gate_ref, rec_ref, cw_ref, cb_ref, wri_ref, br_ref, bi_ref, lam_ref,
                    o_ref, buf_ref, h_ref):
    first = pl.program_id(2) == 0
    c = o_ref.shape[1]
    xr = _causal_conv(buf_ref, rec_ref[...].astype(F32), cw_ref, cb_ref, first)
    pre = jnp.dot(xr.astype(BF16), wri_ref[...], preferred_element_type=F32)
    r = _sigmoid(pre[:, :c] + br_ref[...])
    i = _sigmoid(pre[:, c:] + bi_ref[...])
    log_a = (-RG_C) * r * _softplus(-lam_ref[...])
    a = jnp.exp(log_a)
    u = jnp.sqrt(-jnp.tanh(log_a) * (a * a + 1.0)) * (i * xr)

    @pl.when(first)
    def _():
        h_ref[...] = jnp.zeros_like(h_ref)

    hs, h_last = _scan_rows(a, u, h_ref[...])
    h_ref[...] = h_last
    o_ref[...] = (_gelu_tanh(gate_ref[...].astype(F32)) * hs).astype(BF16)


def rg_core(z, conv_w, conv_b, w_ri, b_r, b_i, lam, bsz, seq, ts=256):
    t = z.shape[0]
    width = conv_w.shape[1]
    nsup = width // RG_SUPER
    spb = seq // ts
    row = lambda c, b, s: b * spb + s
    vec = lambda: pl.BlockSpec((1, RG_SUPER), lambda c, b, s: (0, c))
    return pl.pallas_call(
        _rg_core_kernel,
        grid=(nsup, bsz, spb),
        in_specs=[
            pl.BlockSpec((ts, RG_SUPER), lambda c, b, s: (row(c, b, s), c)),
            pl.BlockSpec((ts, RG_SUPER), lambda c, b, s: (row(c, b, s), nsup + c)),
            pl.BlockSpec((CONV_W, RG_SUPER), lambda c, b, s: (0, c)),
            vec(),
            pl.BlockSpec((None, RG_SUPER, 2 * RG_SUPER), lambda c, b, s: (c, 0, 0)),
            vec(), vec(), vec(),
        ],
        out_specs=pl.BlockSpec((ts, RG_SUPER), lambda c, b, s: (row(c, b, s), c)),
        out_shape=jax.ShapeDtypeStruct((t, width), BF16),
        scratch_shapes=[pltpu.VMEM((ts + SUBLANES, RG_SUPER), F32), pltpu.VMEM((1, RG_SUPER), F32)],
        compiler_params=_cparams(("parallel", "parallel", "arbitrary")),
        name="rg_core",
    )(z, z, conv_w, conv_b.reshape(1, width), w_ri, b_r.reshape(1, width), b_i.reshape(1, width),
      lam.reshape(1, width))


def _rg_gate_weights(w_r, w_i):
    nb, bs, _ = w_r.shape
    per = RG_SUPER // bs
    nsup = nb // per

    def dense(w):
        w4 = w.reshape(nsup, per, bs, bs)
        eye = jnp.eye(per, dtype=w.dtype)
        return jnp.einsum("spcd,pq->spcqd", w4, eye).reshape(nsup, RG_SUPER, RG_SUPER)

    return jnp.concatenate([dense(w_r), dense(w_i)], axis=-1).astype(BF16)


def _sg_in_kernel(x_ref, sc_ref, sh_ref, w_ref, b_ref, lng_ref, lnb_ref, u_ref, v_ref):
    half = u_ref.shape[1]
    h = (x_ref[...] * (1.0 + sc_ref[...]) + sh_ref[...]).astype(BF16)
    zu = jnp.dot(h, w_ref[:, :half], preferred_element_type=F32) + b_ref[:, :half]
    u_ref[...] = _gelu_tanh(zu).astype(BF16)
    zv = jnp.dot(h, w_ref[:, half:], preferred_element_type=F32) + b_ref[:, half:]
    v_ref[...] = _layer_norm_rows(_gelu_tanh(zv), lng_ref[...], lnb_ref[...]).astype(BF16)


def sg_in(x, mod4, w, b, ln_g, ln_b, seq, deps=(), tm=256):
    t, d = x.shape
    half = w.shape[1] // 2
    rows_per_b = seq // tm
    modspec = lambda kk: pl.BlockSpec((None, None, 1, d), lambda i: (i // rows_per_b, kk, 0, 0))
    row = lambda i: (i, 0)
    return pl.pallas_call(
        _after(_sg_in_kernel, 7, deps),
        grid=(t // tm,),
        in_specs=[
            pl.BlockSpec((tm, d), row),
            modspec(1), modspec(0),
            _const_spec((d, 2 * half)),
            _const_spec((1, 2 * half)),
            _const_spec((1, half)), _const_spec((1, half)),
        ] + _dep_specs(deps),
        out_specs=[pl.BlockSpec((tm, half), row), pl.BlockSpec((tm, half), row)],
        out_shape=[jax.ShapeDtypeStruct((t, half), BF16), jax.ShapeDtypeStruct((t, half), BF16)],
        compiler_params=_cparams(("parallel",)),
        name="sg_in",
    )(x, mod4, mod4, w, b.reshape(1, 2 * half), ln_g.reshape(1, half), ln_b.reshape(1, half), *deps)


def _sg_mix_kernel(u_ref, v_ref, w_ref, bt_ref, o_ref):
    nblk = u_ref.shape[0] // SG_BLOCK
    gw = u_ref.shape[1] // SG_GROUPS
    ii = lax.broadcasted_iota(jnp.int32, (SG_BLOCK, SG_BLOCK), 0)
    jj = lax.broadcasted_iota(jnp.int32, (SG_BLOCK, SG_BLOCK), 1)
    allowed = (jj // CHUNK) <= (ii // CHUNK)
    for g in range(SG_GROUPS):
        w = jnp.where(allowed, w_ref[g], 0.0).astype(BF16)
        bias = bt_ref[:, g:g + 1]
        for n in range(nblk):
            rows = slice(n * SG_BLOCK, (n + 1) * SG_BLOCK)
            cols = slice(g * gw, (g + 1) * gw)
            mixed = jnp.dot(w, v_ref[rows, cols], preferred_element_type=F32) + bias
            o_ref[rows, cols] = (u_ref[rows, cols].astype(F32) * mixed).astype(BF16)


def sg_mix(u, v, w_sp, b_sp, tm=256):
    t, half = u.shape
    return pl.pallas_call(
        _sg_mix_kernel,
        grid=(t // tm,),
        in_specs=[
            pl.BlockSpec((tm, half), lambda i: (i, 0)),
            pl.BlockSpec((tm, half), lambda i: (i, 0)),
            _const_spec(w_sp.shape),
            _const_spec((SG_BLOCK, SG_GROUPS)),
        ],
        out_specs=pl.BlockSpec((tm, half), lambda i: (i, 0)),
        out_shape=jax.ShapeDtypeStruct((t, half), BF16),
        compiler_params=_cparams(("parallel",)),
        name="sg_mix",
    )(u, v, w_sp, b_sp.T)


def _ml_qkv_kernel(xm_ref, cw_ref, cb_ref, wq_ref, wk_ref, wv_ref, wg_ref, bg_ref,
                   q_ref, k_ref, v_ref, xc_ref, g_ref, buf_ref):
    first = pl.program_id(1) == 0
    width = xm_ref.shape[1]
    hd = width // ML_HEADS
    xm = xm_ref[...]
    conv = _causal_conv(buf_ref, xm.astype(F32), cw_ref, cb_ref, first)
    xc = (conv * _sigmoid(conv)).astype(BF16)
    xc_ref[...] = xc
    for h in range(ML_HEADS):
        cols = slice(h * hd, (h + 1) * hd)
        q_ref[:, cols] = jnp.dot(xc[:, cols], wq_ref[h], preferred_element_type=F32).astype(BF16)
        k_ref[:, cols] = jnp.dot(xc[:, cols], wk_ref[h], preferred_element_type=F32).astype(BF16)
        v_ref[:, cols] = jnp.dot(xm[:, cols], wv_ref[h], preferred_element_type=F32).astype(BF16)
    gates = (jnp.dot(q_ref[...], wg_ref[0], preferred_element_type=F32)
             + jnp.dot(k_ref[...], wg_ref[1], preferred_element_type=F32)
             + jnp.dot(v_ref[...], wg_ref[2], preferred_element_type=F32) + bg_ref[...])
    is_f = lax.broadcasted_iota(jnp.int32, gates.shape, 1) >= ML_HEADS
    g_ref[...] = jnp.where(is_f, -_softplus(-gates), gates)


def _ml_headwise_weights(w):
    nqb, bs, _ = w.shape
    per = nqb // ML_HEADS
    w4 = w.reshape(ML_HEADS, per, bs, bs)
    eye = jnp.eye(per, dtype=w.dtype)
    return jnp.einsum("hpcd,pq->hpcqd", w4, eye).reshape(ML_HEADS, per * bs, per * bs).astype(BF16)


def ml_qkv(zz, conv_w, conv_b, wq, wk, wv, w_gates, b_gates, bsz, seq, ts=256):
    t = zz.shape[0]
    width = conv_w.shape[1]
    hd = width // ML_HEADS
    spb = seq // ts
    row = lambda b, s: (b * spb + s, 0)
    act = lambda: pl.BlockSpec((ts, width), row)
    return pl.pallas_call(
        _ml_qkv_kernel,
        grid=(bsz, spb),
        in_specs=[
            act(),
            _const_spec((CONV_W, width)), _const_spec((1, width)),
            _const_spec((ML_HEADS, hd, hd)), _const_spec((ML_HEADS, hd, hd)), _const_spec((ML_HEADS, hd, hd)),
            _const_spec((3, width, 2 * ML_HEADS)), _const_spec((1, 2 * ML_HEADS)),
        ],
        out_specs=[act(), act(), act(), act(), pl.BlockSpec((ts, 2 * ML_HEADS), row)],
        out_shape=[jax.ShapeDtypeStruct((t, width), BF16)] * 4
                  + [jax.ShapeDtypeStruct((t, 2 * ML_HEADS), F32)],
        scratch_shapes=[pltpu.VMEM((ts + SUBLANES, width), F32)],
        compiler_params=_cparams(("parallel", "arbitrary")),
        name="ml_qkv",
    )(zz, conv_w, conv_b.reshape(1, width), wq, wk, wv,
      w_gates.reshape(3, width, 2 * ML_HEADS).astype(BF16), b_gates.reshape(1, 2 * ML_HEADS))


def _ml_chunk_kernel(q_ref, k_ref, v_ref, xc_ref, z_ref, gc_ref, gr_ref, ng_ref, skip_ref,
                     o_ref, c_scr, n_scr, m_scr):
    L, width = q_ref.shape
    hd = width // ML_HEADS
    scale = hd ** -0.5

    @pl.when(pl.program_id(1) == 0)
    def _():
        c_scr[...] = jnp.zeros_like(c_scr)
        n_scr[...] = jnp.zeros_like(n_scr)
        m_scr[...] = jnp.zeros_like(m_scr)

    ri = lax.broadcasted_iota(jnp.int32, (L, L), 0)
    ci = lax.broadcasted_iota(jnp.int32, (L, L), 1)
    causal = ri >= ci
    lower = causal.astype(F32)
    upper = (ri <= ci).astype(F32)
    gc = gc_ref[...]
    gr = gr_ref[...]
    bcol = jnp.dot(lower, gc[:, ML_HEADS:], preferred_element_type=F32, precision=lax.Precision.HIGHEST)
    brow = jnp.dot(gr[ML_HEADS:, :], upper, preferred_element_type=F32, precision=lax.Precision.HIGHEST)

    for h in range(ML_HEADS):
        cols = slice(h * hd, (h + 1) * hd)
        bc = bcol[:, h:h + 1]
        br = brow[h:h + 1, :]
        lir = gr[h:h + 1, :]
        lic = gc[:, h:h + 1]
        m_prev = m_scr[h:h + 1, 0:1]
        g = bc + m_prev
        dmat = jnp.where(causal, bc - br + lir, NEG)
        m_t = jnp.maximum(g, jnp.max(dmat, axis=-1, keepdims=True))
        w_intra = jnp.exp(dmat - m_t)
        w_inter = jnp.exp(g - m_t)
        qh = q_ref[:, cols]
        kh = k_ref[:, cols]
        vh = v_ref[:, cols]
        s = lax.dot_general(qh, kh, (((1,), (1,)), ((), ())), preferred_element_type=F32) * scale * w_intra
        c_old = c_scr[h]
        n_old = n_scr[h:h + 1, :]
        num = (jnp.dot(s.astype(BF16), vh, preferred_element_type=F32)
               + w_inter * jnp.dot(qh, c_old.astype(BF16), preferred_element_type=F32))
        den = (jnp.sum(s, axis=-1, keepdims=True)
               + w_inter * jnp.sum(qh.astype(F32) * n_old, axis=-1, keepdims=True))
        out = num / jnp.maximum(jnp.abs(den), jnp.exp(-m_t))

        b_last = bc[L - 1:L, :]
        dec = b_last - bc + lic
        m_new = jnp.maximum(b_last + m_prev, jnp.max(dec, axis=0, keepdims=True))
        ws = jnp.exp(dec - m_new)
        wc = jnp.exp(b_last + m_prev - m_new)
        kw = kh.astype(F32) * (ws * scale)
        c_scr[h] = wc * c_old + lax.dot_general(kw.astype(BF16), vh, (((0,), (0,)), ((), ())),
                                                preferred_element_type=F32)
        n_scr[h:h + 1, :] = wc * n_old + jnp.sum(kw, axis=0, keepdims=True)
        m_scr[h:h + 1, :] = jnp.broadcast_to(m_new, (1, m_scr.shape[1]))

        mu = jnp.mean(out, axis=-1, keepdims=True)
        dlt = out - mu
        var = jnp.mean(dlt * dlt, axis=-1, keepdims=True)
        hn = dlt * lax.rsqrt(var + LN_EPS) * ng_ref[:, cols]
        hn = hn + skip_ref[:, cols] * xc_ref[:, cols].astype(F32)
        o_ref[:, cols] = (_sigmoid(z_ref[:, cols].astype(F32)) * hn).astype(BF16)


def ml_chunk(q, k, v, xc, zz, gates, gates_t, norm_g, skip, bsz, seq):
    t, width = q.shape
    hd = width // ML_HEADS
    L = ML_L
    cpb = seq // L
    row = lambda b, c: (b * cpb + c, 0)
    act = lambda: pl.BlockSpec((L, width), row)
    return pl.pallas_call(
        _ml_chunk_kernel,
        grid=(bsz, cpb),
        in_specs=[
            act(), act(), act(), act(),
            pl.BlockSpec((L, width), lambda b, c: (b * cpb + c, 1)),
            pl.BlockSpec((L, 2 * ML_HEADS), row),
            pl.BlockSpec((2 * ML_HEADS, L), lambda b, c: (0, b * cpb + c)),
            _const_spec((1, width)), _const_spec((1, width)),
        ],
        out_specs=act(),
        out_shape=jax.ShapeDtypeStruct((t, width), BF16),
        scratch_shapes=[pltpu.VMEM((ML_HEADS, hd, hd), F32), pltpu.VMEM((ML_HEADS, hd), F32),
                        pltpu.VMEM((ML_HEADS, 128), F32)],
        compiler_params=_cparams(("parallel", "arbitrary")),
        name="ml_chunk",
    )(q, k, v, xc, zz, gates, gates_t, norm_g.reshape(1, width), skip.reshape(1, width))


def _router_kernel(h_ref, w_ref, b_ref, eid_ref, prob_ref, rank_ref, cnt_ref, cnt_scr):
    tr = h_ref.shape[0]
    ne = w_ref.shape[0]

    @pl.when(pl.program_id(0) == 0)
    def _():
        cnt_scr[...] = jnp.zeros_like(cnt_scr)

    logits = lax.dot_general(w_ref[...], _unpack_rows(h_ref[...]), (((1,), (1,)), ((), ())),
                             preferred_element_type=F32) + b_ref[...]
    sub = lax.broadcasted_iota(jnp.int32, (ne, tr), 0)
    work = logits
    sel = jnp.zeros((ne, tr), F32)
    hits = []
    top = None
    for k in range(TOP_K):
        m = jnp.max(work, axis=0, keepdims=True)
        idx = jnp.min(jnp.where(work == m, sub, ne), axis=0, keepdims=True)
        hit = sub == idx
        hits.append((hit, idx))
        sel = jnp.where(hit, 1.0, sel)
        work = jnp.where(hit, -jnp.inf, work)
        if k == 0:
            top = m
    e = sel * jnp.exp(logits - top)
    p = e / jnp.sum(e, axis=0, keepdims=True)

    ri = lax.broadcasted_iota(jnp.int32, (tr, tr), 0)
    ci = lax.broadcasted_iota(jnp.int32, (tr, tr), 1)
    strict_upper = (ri < ci).astype(BF16)
    before = jnp.dot(sel.astype(BF16), strict_upper, preferred_element_type=F32)
    rank = cnt_scr[...] + before
    cnt_scr[...] = cnt_scr[...] + jnp.sum(sel, axis=1, keepdims=True)
    cnt_ref[...] = cnt_scr[...].astype(jnp.int32)

    row = lax.broadcasted_iota(jnp.int32, (SUBLANES, tr), 0)
    eid = jnp.zeros((SUBLANES, tr), jnp.int32)
    prob = jnp.zeros((SUBLANES, tr), F32)
    rnk = jnp.zeros((SUBLANES, tr), F32)
    for k, (hit, idx) in enumerate(hits):
        slot = row == k
        eid = jnp.where(slot, idx, eid)
        prob = jnp.where(slot, jnp.sum(jnp.where(hit, p, 0.0), axis=0, keepdims=True), prob)
        rnk = jnp.where(slot, jnp.sum(jnp.where(hit, rank, 0.0), axis=0, keepdims=True), rnk)
    eid_ref[...] = eid
    prob_ref[...] = prob
    rank_ref[...] = rnk.astype(jnp.int32)


def router(h2, w_router, b_router, tr=512):
    t, dp = h2.shape
    d, ne = w_router.shape
    tr = min(tr, t)
    col = lambda i: (0, i)
    return pl.pallas_call(
        _router_kernel,
        grid=(t // tr,),
        in_specs=[pl.BlockSpec((tr, dp), lambda i: (i, 0)), _const_spec((ne, d)), _const_spec((ne, 1))],
        out_specs=[pl.BlockSpec((SUBLANES, tr), col), pl.BlockSpec((SUBLANES, tr), col),
                   pl.BlockSpec((SUBLANES, tr), col), pl.BlockSpec((ne, 1), lambda i: (0, 0))],
        out_shape=[jax.ShapeDtypeStruct((SUBLANES, t), jnp.int32), jax.ShapeDtypeStruct((SUBLANES, t), F32),
                   jax.ShapeDtypeStruct((SUBLANES, t), jnp.int32), jax.ShapeDtypeStruct((ne, 1), jnp.int32)],
        scratch_shapes=[pltpu.VMEM((ne, 1), F32)],
        compiler_params=_cparams(("arbitrary",)),
        name="router",
    )(h2, w_router.T.astype(BF16), b_router.reshape(ne, 1))


def _expert_kernel(te_ref, nu_ref, nx_ref, x_ref, wgu_hbm, bgu_ref, wd_hbm, bd_ref, o_ref,
                   wgu_f32, wd_f32, wgu_bf, wd_bf, sem, *, layer):
    i = pl.program_id(0)
    used = i < nu_ref[0]
    e = te_ref[i]
    new_expert = jnp.logical_or(i == 0, e != te_ref[jnp.maximum(i - 1, 0)])

    def fetch(expert):
        return (pltpu.make_async_copy(wgu_hbm.at[layer, expert], wgu_f32, sem.at[0]),
                pltpu.make_async_copy(wd_hbm.at[layer, expert], wd_f32, sem.at[1]))

    @pl.when(jnp.logical_and(used, i == 0))
    def _():
        for cp in fetch(e):
            cp.start()

    @pl.when(jnp.logical_and(used, new_expert))
    def _():
        for cp in fetch(e):
            cp.wait()
        wgu_bf[...] = wgu_f32[...].astype(BF16)
        wd_bf[...] = wd_f32[...].astype(BF16)
        nxt = nx_ref[i]

        @pl.when(nxt != e)
        def _():
            for cp in fetch(nxt):
                cp.start()

    @pl.when(used)
    def _():
        f = wd_bf.shape[0]
        gu = jnp.dot(_unpack_rows(x_ref[...]), wgu_bf[...], preferred_element_type=F32) + bgu_ref[...]
        gate = jnp.minimum(gu[:, :f], SWIGLU_LIMIT)
        up = jnp.clip(gu[:, f:], -SWIGLU_LIMIT, SWIGLU_LIMIT)
        act = gate * _sigmoid(SWIGLU_ALPHA * gate) * (up + 1.0)
        y = jnp.dot(act.astype(BF16), wd_bf[...], preferred_element_type=F32) + bd_ref[...]
        o_ref[...] = _pack_rows(y)

    @pl.when(jnp.logical_not(used))
    def _():
        o_ref[...] = jnp.zeros_like(o_ref)


def experts(xs, tile_e, n_used, tile_next, w_gu, b_gu, w_d, b_d, layer, deps=()):
    p, dp = xs.shape
    depth, ne, d, f2 = w_gu.shape
    f = f2 // 2
    te = EXPERT_TILE
    grid_spec = pltpu.PrefetchScalarGridSpec(
        num_scalar_prefetch=3,
        grid=(p // te,),
        in_specs=[
            pl.BlockSpec((te, dp), lambda i, e, n, nx: (jnp.minimum(i, jnp.maximum(n[0] - 1, 0)), 0)),
            pl.BlockSpec(memory_space=pl.ANY),
            pl.BlockSpec((None, None, 1, f2), lambda i, e, n, nx: (layer, e[i], 0, 0)),
            pl.BlockSpec(memory_space=pl.ANY),
            pl.BlockSpec((None, None, 1, d), lambda i, e, n, nx: (layer, e[i], 0, 0)),
        ] + _dep_specs(deps),
        out_specs=pl.BlockSpec((te, dp), lambda i, e, n, nx: (i, 0)),
        scratch_shapes=[pltpu.VMEM((d, f2), F32), pltpu.VMEM((f, d), F32),
                        pltpu.VMEM((d, f2), BF16), pltpu.VMEM((f, d), BF16),
                        pltpu.SemaphoreType.DMA((2,))],
    )
    return pl.pallas_call(
        _after(functools.partial(_expert_kernel, layer=layer), 8, deps),
        grid_spec=grid_spec,
        out_shape=jax.ShapeDtypeStruct((p, dp), jnp.uint32),
        compiler_params=_cparams(("arbitrary",)),
        name="experts",
    )(tile_e, n_used, tile_next, xs, w_gu, b_gu.reshape(depth, ne, 1, f2), w_d,
      b_d.reshape(depth, ne, 1, d), *deps)


def _combine_ln_kernel(y0_ref, y1_ref, y2_ref, y3_ref, p_ref, x_ref, g_ref, lng_ref, lnb_ref, xo_ref,
                       *, alpha):
    p = p_ref[...]
    y = None
    for k, y_ref in enumerate((y0_ref, y1_ref, y2_ref, y3_ref)):
        term = p[:, k:k + 1] * _unpack_rows(y_ref[...]).astype(F32)
        y = term if y is None else y + term
    v = alpha * x_ref[...] + (1.0 + g_ref[...]) * y
    xo_ref[...] = _layer_norm_rows(v, lng_ref[...], lnb_ref[...])


def combine_ln(ys, prob, x, mod4, ln_g, ln_b, seq, alpha, deps=(), tm=256):
    t, d = x.shape
    ne = prob.shape[1]
    rows_per_b = seq // tm
    row = lambda i: (i, 0)
    return pl.pallas_call(
        _after(functools.partial(_combine_ln_kernel, alpha=alpha), 9, deps),
        grid=(t // tm,),
        in_specs=[pl.BlockSpec((tm, d // 2), row)] * 4 + [
            pl.BlockSpec((tm, ne), row),
            pl.BlockSpec((tm, d), row),
            pl.BlockSpec((None, None, 1, d), lambda i: (i // rows_per_b, 5, 0, 0)),
            _const_spec((1, d)), _const_spec((1, d)),
        ] + _dep_specs(deps),
        out_specs=pl.BlockSpec((tm, d), row),
        out_shape=jax.ShapeDtypeStruct((t, d), F32),
        compiler_params=_cparams(("parallel",)),
        name="combine_ln",
    )(*ys, prob, x, mod4, ln_g.reshape(1, d), ln_b.reshape(1, d), *deps)


def moe_route(h2, w_router, b_router):
    t, d = h2.shape
    ne = w_router.shape[1]
    te = EXPERT_TILE
    eid, prob, rank, counts2d = router(h2, w_router, b_router)
    counts = counts2d[:, 0]
    padded = ((counts + te - 1) // te) * te
    ends = jnp.cumsum(padded)
    starts = ends - padded
    prob = prob.T
    eid4 = eid[:TOP_K]
    first_slot = jnp.zeros_like(eid4)
    for e in range(ne):
        first_slot = jnp.where(eid4 == e, starts[e], first_slot)
    dest4 = first_slot + rank[:TOP_K]
    n_slots = t * TOP_K + ne * te
    tok = jnp.broadcast_to(jnp.arange(t, dtype=jnp.int32)[None, :], (TOP_K, t))
    src = jnp.zeros((n_slots,), jnp.int32).at[dest4.reshape(-1)].set(
        tok.reshape(-1), unique_indices=True, mode=_IN_BOUNDS)
    n_tiles = n_slots // te
    tile_start = jnp.arange(n_tiles, dtype=jnp.int32) * te
    tile_e = jnp.sum((ends[None, :] <= tile_start[:, None]).astype(jnp.int32), axis=1)
    n_used = (ends[-1] // te).astype(jnp.int32).reshape(1)
    last_e = jnp.sum((ends <= ends[-1] - 1).astype(jnp.int32))
    tile_e = jnp.where(tile_start < ends[-1], tile_e, last_e).astype(jnp.int32)
    ids = jnp.arange(ne, dtype=jnp.int32)
    later = jnp.where((ids[None, :] > ids[:, None]) & (padded[None, :] > 0), ids[None, :], ne)
    nxt = jnp.min(later, axis=1)
    nxt = jnp.where(nxt == ne, ids, nxt)
    tile_next = jnp.sum(jnp.where(tile_e[:, None] == ids[None, :], nxt[None, :], 0), axis=1).astype(jnp.int32)
    return dict(src=src, dest4=dest4, prob=prob, tile_e=tile_e, n_used=n_used, tile_next=tile_next,
                order_token=counts2d)


def kernel(x, c, ada_w, ada_b, ln1_g, ln1_b, ln2_g, ln2_b, rg_w_in, rg_conv_w, rg_conv_b, rg_w_rgate, rg_b_rgate, rg_w_igate, rg_b_igate, rg_lam, rg_w_out, sg_w_in, sg_b_in, sg_ln_g, sg_ln_b, sg_w_sp, sg_b_sp, sg_w_out, sg_b_out, ml_w_in, ml_conv_w, ml_conv_b, ml_w_q, ml_w_k, ml_w_v, ml_w_gates, ml_b_gates, ml_skip, ml_norm_g, ml_w_out, moe_w_router, moe_b_router, moe_w_gate_up, moe_b_gate_up, moe_w_down, moe_b_down):
    bsz, seq, d = x.shape
    depth = ada_w.shape[0]
    alpha = (2 * depth) ** 0.25
    xfull = x.reshape(bsz * seq, d)
    mod = ada_mod(c, ada_w, ada_b)
    zero_bias = jnp.zeros((d,), F32)

    groups = BATCH_GROUPS if bsz % BATCH_GROUPS == 0 else 1
    gb = bsz // groups
    xs = [xfull] * groups
    offs = [g * gb * seq for g in range(groups)]

    for i in range(depth):
        kind, j = i % 3, i // 3
        if kind == 0:
            w_in = rg_w_in[j].astype(BF16)
            w_ri = _rg_gate_weights(rg_w_rgate[j], rg_w_igate[j])
            w_out, b_out = rg_w_out[j].astype(BF16), zero_bias
        elif kind == 1:
            w_in = sg_w_in[j].astype(BF16)
            w_out, b_out = sg_w_out[j].astype(BF16), sg_b_out[j]
        else:
            w_in = ml_w_in[j].astype(BF16)
            wq, wk, wv = (_ml_headwise_weights(w[j]) for w in (ml_w_q, ml_w_k, ml_w_v))
            w_out, b_out = ml_w_out[j].astype(BF16), zero_bias
        mods, routes, x_mid, h_mid = [], [], [], []
        prev = ()
        for g in range(groups):
            mod4 = mod[i, g * gb:(g + 1) * gb].reshape(gb, 6, 1, d)
            xt, off = xs[g], offs[g]
            if kind == 0:
                z = mm_in(xt, mod4, 1, 0, w_in, seq, off, deps=prev)
                a = rg_core(z, rg_conv_w[j], rg_conv_b[j], w_ri, rg_b_rgate[j], rg_b_igate[j], rg_lam[j],
                            gb, seq)
            elif kind == 1:
                u, v = sg_in(xt, mod4, w_in, sg_b_in[j], sg_ln_g[j], sg_ln_b[j], seq, deps=prev)
                a = sg_mix(u, v, sg_w_sp[j], sg_b_sp[j])
            else:
                zz = mm_in(xt, mod4, 1, 0, w_in, seq, off, deps=prev)
                q, k, v, xc, gates = ml_qkv(zz, ml_conv_w[j], ml_conv_b[j], wq, wk, wv,
                                            ml_w_gates[j], ml_b_gates[j], gb, seq)
                a = ml_chunk(q, k, v, xc, zz, gates, gates.T, ml_norm_g[j], ml_skip[j], gb, seq)
            xt, h2 = mm_res_ln(a, w_out, b_out, xt, mod4, 2, 4, 3, ln1_g[i], ln1_b[i], seq, alpha, off)
            route = moe_route(h2, moe_w_router[i], moe_b_router[i])
            prev = (route["order_token"],)
            mods.append(mod4); routes.append(route); x_mid.append(xt); h_mid.append(h2)
        ysorted = []
        for g in range(groups):
            r = routes[g]
            rows = h_mid[g].at[r["src"]].get(mode=_IN_BOUNDS)
            ysorted.append(experts(rows, r["tile_e"], r["n_used"], r["tile_next"], moe_w_gate_up,
                                   moe_b_gate_up, moe_w_down, moe_b_down, i, deps=prev))
            prev = (ysorted[g],)
        for g in range(groups):
            r = routes[g]
            ys = [ysorted[g].at[r["dest4"][k]].get(mode=_IN_BOUNDS) for k in range(TOP_K)]
            xs[g] = combine_ln(ys, r["prob"], x_mid[g], mods[g], ln2_g[i], ln2_b[i], seq, alpha, deps=prev)
            prev = (xs[g],)
            offs[g] = 0
    return jnp.concatenate(xs, axis=0).reshape(bsz, seq, d)
```

```python
import functools
import math

import jax
import jax.numpy as jnp
from jax import lax
from jax.experimental import pallas as pl
from jax.experimental.pallas import tpu as pltpu

F32 = jnp.float32
BF16 = jnp.bfloat16

LN_EPS = 1e-5
RG_C = 8.0
CONV_W = 4
CHUNK = 64
SG_BLOCK = 128
SG_GROUPS = 8
ML_HEADS = 8
TOP_K = 4
SWIGLU_LIMIT = 7.0
SWIGLU_ALPHA = 1.702

SUBLANES = 8
RG_SUPER = 640
ML_L = 256
EXPERT_TILE = 512
BATCH_GROUPS = 1
VMEM_LIMIT = 56 * 1024 * 1024
NEG = -1e30
_IN_BOUNDS = "promise_in_bounds"


def _cparams(sem):
    return pltpu.CompilerParams(dimension_semantics=sem, vmem_limit_bytes=VMEM_LIMIT)


def _sigmoid(x):
    return 1.0 / (1.0 + jnp.exp(-x))


def _gelu_tanh(x):
    return 0.5 * x * (1.0 + jnp.tanh(math.sqrt(2.0 / math.pi) * (x + 0.044715 * (x * x * x))))


def _softplus(x):
    return jnp.maximum(x, 0.0) + jnp.log1p(jnp.exp(-jnp.abs(x)))


def _layer_norm_rows(v, g, b):
    mu = jnp.mean(v, axis=-1, keepdims=True)
    d = v - mu
    var = jnp.mean(d * d, axis=-1, keepdims=True)
    return d * lax.rsqrt(var + LN_EPS) * g + b


def _pack_rows(v):
    n = v.shape[1] // 2
    lo = lax.bitcast_convert_type(v[:, :n].astype(BF16).astype(F32), jnp.uint32)
    hi = lax.bitcast_convert_type(v[:, n:].astype(BF16).astype(F32), jnp.uint32)
    return (lo >> 16) | hi


def _unpack_rows(w):
    lo = lax.bitcast_convert_type(w << 16, F32)
    hi = lax.bitcast_convert_type(w & jnp.uint32(0xFFFF0000), F32)
    return jnp.concatenate([lo.astype(BF16), hi.astype(BF16)], axis=1)


def _const_spec(shape):
    nd = len(shape)
    return pl.BlockSpec(shape, lambda *_: (0,) * nd, pipeline_mode=pl.Buffered(1))


def _after(kernel_fn, n_in, deps):
    nd = len(deps)
    if nd == 0:
        return kernel_fn

    def body(*refs):
        return kernel_fn(*refs[:n_in], *refs[n_in + nd:])

    return body


def _dep_specs(deps):
    return [pl.BlockSpec(memory_space=pl.ANY)] * len(deps)


def _ada_kernel(c_ref, w_ref, b_ref, o_ref):
    c = c_ref[...]
    cond = c * _sigmoid(c)
    o_ref[...] = jnp.dot(cond.astype(BF16), w_ref[...].astype(BF16),
                         preferred_element_type=F32) + b_ref[...]


def ada_mod(c, ada_w, ada_b, tn=1024):
    depth, d, n = ada_w.shape
    bsz = c.shape[0]
    return pl.pallas_call(
        _ada_kernel,
        grid=(depth, n // tn),
        in_specs=[
            pl.BlockSpec((bsz, d), lambda l, j: (0, 0)),
            pl.BlockSpec((None, d, tn), lambda l, j: (l, 0, j)),
            pl.BlockSpec((None, 1, tn), lambda l, j: (l, 0, j)),
        ],
        out_specs=pl.BlockSpec((None, bsz, tn), lambda l, j: (l, 0, j)),
        out_shape=jax.ShapeDtypeStruct((depth, bsz, n), F32),
        compiler_params=_cparams(("parallel", "parallel")),
        name="ada_mod",
    )(c, ada_w, ada_b.reshape(depth, 1, n))


def _mm_in_kernel(x_ref, sc_ref, sh_ref, w_ref, o_ref, *, tn):
    h = (x_ref[...] * (1.0 + sc_ref[...]) + sh_ref[...]).astype(BF16)
    for j in range(o_ref.shape[1] // tn):
        cols = slice(j * tn, (j + 1) * tn)
        o_ref[:, cols] = jnp.dot(h, w_ref[:, cols], preferred_element_type=F32).astype(o_ref.dtype)


def mm_in(x, mod4, k_sc, k_sh, w, seq, row_off=0, deps=(), tm=512, tn=1024):
    t, d = mod4.shape[0] * seq, x.shape[1]
    n = w.shape[1]
    tm = min(tm, seq)
    rows_per_b = seq // tm
    blk_off = row_off // tm
    return pl.pallas_call(
        _after(functools.partial(_mm_in_kernel, tn=tn), 4, deps),
        grid=(t // tm,),
        in_specs=[
            pl.BlockSpec((tm, d), lambda i: (i + blk_off, 0)),
            pl.BlockSpec((None, None, 1, d), lambda i: (i // rows_per_b, k_sc, 0, 0)),
            pl.BlockSpec((None, None, 1, d), lambda i: (i // rows_per_b, k_sh, 0, 0)),
            _const_spec((d, n)),
        ] + _dep_specs(deps),
        out_specs=pl.BlockSpec((tm, n), lambda i: (i, 0)),
        out_shape=jax.ShapeDtypeStruct((t, n), BF16),
        compiler_params=_cparams(("parallel",)),
        name="mm_in",
    )(x, mod4, mod4, w, *deps)


def _mm_res_ln_kernel(a_ref, w_ref, b_ref, x_ref, g_ref, lng_ref, lnb_ref, sc_ref, sh_ref,
                      xo_ref, ho_ref, *, alpha):
    y = jnp.dot(a_ref[...], w_ref[...], preferred_element_type=F32) + b_ref[...]
    v = alpha * x_ref[...] + (1.0 + g_ref[...]) * y
    xn = _layer_norm_rows(v, lng_ref[...], lnb_ref[...])
    xo_ref[...] = xn
    ho_ref[...] = _pack_rows(xn * (1.0 + sc_ref[...]) + sh_ref[...])


def mm_res_ln(a, w, bias, x, mod4, k_gate, k_sc, k_sh, ln_g, ln_b, seq, alpha, row_off=0, tm=512):
    t, k = a.shape
    d = x.shape[1]
    rows_per_b = seq // tm
    blk_off = row_off // tm
    modspec = lambda kk: pl.BlockSpec((None, None, 1, d), lambda i: (i // rows_per_b, kk, 0, 0))
    return pl.pallas_call(
        functools.partial(_mm_res_ln_kernel, alpha=alpha),
        grid=(t // tm,),
        in_specs=[
            pl.BlockSpec((tm, k), lambda i: (i, 0)),
            _const_spec((k, d)),
            _const_spec((1, d)),
            pl.BlockSpec((tm, d), lambda i: (i + blk_off, 0)),
            modspec(k_gate),
            _const_spec((1, d)),
            _const_spec((1, d)),
            modspec(k_sc),
            modspec(k_sh),
        ],
        out_specs=[pl.BlockSpec((tm, d), lambda i: (i, 0)), pl.BlockSpec((tm, d // 2), lambda i: (i, 0))],
        out_shape=[jax.ShapeDtypeStruct((t, d), F32), jax.ShapeDtypeStruct((t, d // 2), jnp.uint32)],
        compiler_params=_cparams(("parallel",)),
        name="mm_res_ln",
    )(a, w, bias.reshape(1, d), x, mod4, ln_g.reshape(1, d), ln_b.reshape(1, d), mod4, mod4)


def _causal_conv(buf_ref, x, w_ref, b_ref, first):
    ts = x.shape[0]

    @pl.when(first)
    def _():
        buf_ref[0:SUBLANES, :] = jnp.zeros((SUBLANES, x.shape[1]), F32)

    buf_ref[SUBLANES:SUBLANES + ts, :] = x
    y = b_ref[...] + w_ref[CONV_W - 1:CONV_W, :] * x
    for back in range(1, CONV_W):
        y = y + w_ref[CONV_W - 1 - back:CONV_W - back, :] * buf_ref[SUBLANES - back:SUBLANES - back + ts, :]
    buf_ref[0:SUBLANES, :] = buf_ref[ts:ts + SUBLANES, :]
    return y


def _scan_rows(a, u, h0):
    ts, c = a.shape
    groups = ts // SUBLANES
    a3 = a.reshape(groups, SUBLANES, c)
    u3 = u.reshape(groups, SUBLANES, c)
    rid = lax.broadcasted_iota(jnp.int32, a3.shape, 1)
    for k in (1, 2, 4):
        keep = rid >= k
        a_sh = jnp.where(keep, pltpu.roll(a3, k, 1), 1.0)
        u_sh = jnp.where(keep, pltpu.roll(u3, k, 1), 0.0)
        u3 = u3 + a3 * u_sh
        a3 = a3 * a_sh
    h = h0
    outs = []
    for j in range(groups):
        hj = u3[j] + a3[j] * h
        outs.append(hj)
        h = hj[SUBLANES - 1:SUBLANES, :]
    return jnp.concatenate(outs, axis=0), h


def _rg_core_kernel(gate_ref, rec_ref, cw_ref, cb_ref, wri_ref, br_ref, bi_ref, lam_ref,
                    o_ref, buf_ref, h_ref):
    first = pl.program_id(2) == 0
    c = o_ref.shape[1]
    xr = _causal_conv(buf_ref, rec_ref[...].astype(F32), cw_ref, cb_ref, first)
    pre = jnp.dot(xr.astype(BF16), wri_ref[...], preferred_element_type=F32)
    r = _sigmoid(pre[:, :c] + br_ref[...])
    i = _sigmoid(pre[:, c:] + bi_ref[...])
    log_a = (-RG_C) * r * _softplus(-lam_ref[...])
    a = jnp.exp(log_a)
    u = jnp.sqrt(-jnp.tanh(log_a) * (a * a + 1.0)) * (i * xr)

    @pl.when(first)
    def _():
        h_ref[...] = jnp.zeros_like(h_ref)

    hs, h_last = _scan_rows(a, u, h_ref[...])
    h_ref[...] = h_last
    o_ref[...] = (_gelu_tanh(gate_ref[...].astype(F32)) * hs).astype(BF16)


def rg_core(z, conv_w, conv_b, w_ri, b_r, b_i, lam, bsz, seq, ts=256):
    t = z.shape[0]
    width = conv_w.shape[1]
    nsup = width // RG_SUPER
    spb = seq // ts
    row = lambda c, b, s: b * spb + s
    vec = lambda: pl.BlockSpec((1, RG_SUPER), lambda c, b, s: (0, c))
    return pl.pallas_call(
        _rg_core_kernel,
        grid=(nsup, bsz, spb),
        in_specs=[
            pl.BlockSpec((ts, RG_SUPER), lambda c, b, s: (row(c, b, s), c)),
            pl.BlockSpec((ts, RG_SUPER), lambda c, b, s: (row(c, b, s), nsup + c)),
            pl.BlockSpec((CONV_W, RG_SUPER), lambda c, b, s: (0, c)),
            vec(),
            pl.BlockSpec((None, RG_SUPER, 2 * RG_SUPER), lambda c, b, s: (c, 0, 0)),
            vec(), vec(), vec(),
        ],
        out_specs=pl.BlockSpec((ts, RG_SUPER), lambda c, b, s: (row(c, b, s), c)),
        out_shape=jax.ShapeDtypeStruct((t, width), BF16),
        scratch_shapes=[pltpu.VMEM((ts + SUBLANES, RG_SUPER), F32), pltpu.VMEM((1, RG_SUPER), F32)],
        compiler_params=_cparams(("parallel", "parallel", "arbitrary")),
        name="rg_core",
    )(z, z, conv_w, conv_b.reshape(1, width), w_ri, b_r.reshape(1, width), b_i.reshape(1, width),
      lam.reshape(1, width))


def _rg_gate_weights(w_r, w_i):
    nb, bs, _ = w_r.shape
    per = RG_SUPER // bs
    nsup = nb // per

    def dense(w):
        w4 = w.reshape(nsup, per, bs, bs)
        eye = jnp.eye(per, dtype=w.dtype)
        return jnp.einsum("spcd,pq->spcqd", w4, eye).reshape(nsup, RG_SUPER, RG_SUPER)

    return jnp.concatenate([dense(w_r), dense(w_i)], axis=-1).astype(BF16)


def _sg_in_kernel(x_ref, sc_ref, sh_ref, w_ref, b_ref, lng_ref, lnb_ref, u_ref, v_ref):
    half = u_ref.shape[1]
    h = (x_ref[...] * (1.0 + sc_ref[...]) + sh_ref[...]).astype(BF16)
    zu = jnp.dot(h, w_ref[:, :half], preferred_element_type=F32) + b_ref[:, :half]
    u_ref[...] = _gelu_tanh(zu).astype(BF16)
    zv = jnp.dot(h, w_ref[:, half:], preferred_element_type=F32) + b_ref[:, half:]
    v_ref[...] = _layer_norm_rows(_gelu_tanh(zv), lng_ref[...], lnb_ref[...]).astype(BF16)


def sg_in(x, mod4, w, b, ln_g, ln_b, seq, deps=(), tm=512):
    t, d = x.shape
    half = w.shape[1] // 2
    rows_per_b = seq // tm
    modspec = lambda kk: pl.BlockSpec((None, None, 1, d), lambda i: (i // rows_per_b, kk, 0, 0))
    row = lambda i: (i, 0)
    return pl.pallas_call(
        _after(_sg_in_kernel, 7, deps),
        grid=(t // tm,),
        in_specs=[
            pl.BlockSpec((tm, d), row),
            modspec(1), modspec(0),
            _const_spec((d, 2 * half)),
            _const_spec((1, 2 * half)),
            _const_spec((1, half)), _const_spec((1, half)),
        ] + _dep_specs(deps),
        out_specs=[pl.BlockSpec((tm, half), row), pl.BlockSpec((tm, half), row)],
        out_shape=[jax.ShapeDtypeStruct((t, half), BF16), jax.ShapeDtypeStruct((t, half), BF16)],
        compiler_params=_cparams(("parallel",)),
        name="sg_in",
    )(x, mod4, mod4, w, b.reshape(1, 2 * half), ln_g.reshape(1, half), ln_b.reshape(1, half), *deps)


def _sg_mix_kernel(u_ref, v_ref, w_ref, bt_ref, o_ref):
    nblk = u_ref.shape[0] // SG_BLOCK
    gw = u_ref.shape[1] // SG_GROUPS
    ii = lax.broadcasted_iota(jnp.int32, (SG_BLOCK, SG_BLOCK), 0)
    jj = lax.broadcasted_iota(jnp.int32, (SG_BLOCK, SG_BLOCK), 1)
    allowed = (jj // CHUNK) <= (ii // CHUNK)
    for g in range(SG_GROUPS):
        w = jnp.where(allowed, w_ref[g], 0.0).astype(BF16)
        bias = bt_ref[:, g:g + 1]
        for n in range(nblk):
            rows = slice(n * SG_BLOCK, (n + 1) * SG_BLOCK)
            cols = slice(g * gw, (g + 1) * gw)
            mixed = jnp.dot(w, v_ref[rows, cols], preferred_element_type=F32) + bias
            o_ref[rows, cols] = (u_ref[rows, cols].astype(F32) * mixed).astype(BF16)


def sg_mix(u, v, w_sp, b_sp, tm=256):
    t, half = u.shape
    return pl.pallas_call(
        _sg_mix_kernel,
        grid=(t // tm,),
        in_specs=[
            pl.BlockSpec((tm, half), lambda i: (i, 0)),
            pl.BlockSpec((tm, half), lambda i: (i, 0)),
            _const_spec(w_sp.shape),
            _const_spec((SG_BLOCK, SG_GROUPS)),
        ],
        out_specs=pl.BlockSpec((tm, half), lambda i: (i, 0)),
        out_shape=jax.ShapeDtypeStruct((t, half), BF16),
        compiler_params=_cparams(("parallel",)),
        name="sg_mix",
    )(u, v, w_sp, b_sp.T)


def _ml_qkv_kernel(xm_ref, cw_ref, cb_ref, wq_ref, wk_ref, wv_ref, wg_ref, bg_ref,
                   q_ref, k_ref, v_ref, xc_ref, g_ref, buf_ref):
    first = pl.program_id(1) == 0
    width = xm_ref.shape[1]
    hd = width // ML_HEADS
    xm = xm_ref[...]
    conv = _causal_conv(buf_ref, xm.astype(F32), cw_ref, cb_ref, first)
    xc = (conv * _sigmoid(conv)).astype(BF16)
    xc_ref[...] = xc
    for h in range(ML_HEADS):
        cols = slice(h * hd, (h + 1) * hd)
        q_ref[:, cols] = jnp.dot(xc[:, cols], wq_ref[h], preferred_element_type=F32).astype(BF16)
        k_ref[:, cols] = jnp.dot(xc[:, cols], wk_ref[h], preferred_element_type=F32).astype(BF16)
        v_ref[:, cols] = jnp.dot(xm[:, cols], wv_ref[h], preferred_element_type=F32).astype(BF16)
    gates = (jnp.dot(q_ref[...], wg_ref[0], preferred_element_type=F32)
             + jnp.dot(k_ref[...], wg_ref[1], preferred_element_type=F32)
             + jnp.dot(v_ref[...], wg_ref[2], preferred_element_type=F32) + bg_ref[...])
    is_f = lax.broadcasted_iota(jnp.int32, gates.shape, 1) >= ML_HEADS
    g_ref[...] = jnp.where(is_f, -_softplus(-gates), gates)


def _ml_headwise_weights(w):
    nqb, bs, _ = w.shape
    per = nqb // ML_HEADS
    w4 = w.reshape(ML_HEADS, per, bs, bs)
    eye = jnp.eye(per, dtype=w.dtype)
    return jnp.einsum("hpcd,pq->hpcqd", w4, eye).reshape(ML_HEADS, per * bs, per * bs).astype(BF16)


def ml_qkv(zz, conv_w, conv_b, wq, wk, wv, w_gates, b_gates, bsz, seq, ts=256):
    t = zz.shape[0]
    width = conv_w.shape[1]
    hd = width // ML_HEADS
    spb = seq // ts
    row = lambda b, s: (b * spb + s, 0)
    act = lambda: pl.BlockSpec((ts, width), row)
    return pl.pallas_call(
        _ml_qkv_kernel,
        grid=(bsz, spb),
        in_specs=[
            act(),
            _const_spec((CONV_W, width)), _const_spec((1, width)),
            _const_spec((ML_HEADS, hd, hd)), _const_spec((ML_HEADS, hd, hd)), _const_spec((ML_HEADS, hd, hd)),
            _const_spec((3, width, 2 * ML_HEADS)), _const_spec((1, 2 * ML_HEADS)),
        ],
        out_specs=[act(), act(), act(), act(), pl.BlockSpec((ts, 2 * ML_HEADS), row)],
        out_shape=[jax.ShapeDtypeStruct((t, width), BF16)] * 4
                  + [jax.ShapeDtypeStruct((t, 2 * ML_HEADS), F32)],
        scratch_shapes=[pltpu.VMEM((ts + SUBLANES, width), F32)],
        compiler_params=_cparams(("parallel", "arbitrary")),
        name="ml_qkv",
    )(zz, conv_w, conv_b.reshape(1, width), wq, wk, wv,
      w_gates.reshape(3, width, 2 * ML_HEADS).astype(BF16), b_gates.reshape(1, 2 * ML_HEADS))


def _ml_chunk_kernel(q_ref, k_ref, v_ref, xc_ref, z_ref, gc_ref, gr_ref, ng_ref, skip_ref,
                     o_ref, c_scr, n_scr, m_scr):
    L, width = q_ref.shape
    hd = width // ML_HEADS
    scale = hd ** -0.5

    @pl.when(pl.program_id(1) == 0)
    def _():
        c_scr[...] = jnp.zeros_like(c_scr)
        n_scr[...] = jnp.zeros_like(n_scr)
        m_scr[...] = jnp.zeros_like(m_scr)

    ri = lax.broadcasted_iota(jnp.int32, (L, L), 0)
    ci = lax.broadcasted_iota(jnp.int32, (L, L), 1)
    causal = ri >= ci
    lower = causal.astype(F32)
    upper = (ri <= ci).astype(F32)
    gc = gc_ref[...]
    gr = gr_ref[...]
    bcol = jnp.dot(lower, gc[:, ML_HEADS:], preferred_element_type=F32, precision=lax.Precision.HIGHEST)
    brow = jnp.dot(gr[ML_HEADS:, :], upper, preferred_element_type=F32, precision=lax.Precision.HIGHEST)

    for h in range(ML_HEADS):
        cols = slice(h * hd, (h + 1) * hd)
        bc = bcol[:, h:h + 1]
        br = brow[h:h + 1, :]
        lir = gr[h:h + 1, :]
        lic = gc[:, h:h + 1]
        m_prev = m_scr[h:h + 1, 0:1]
        g = bc + m_prev
        dmat = jnp.where(causal, bc - br + lir, NEG)
        m_t = jnp.maximum(g, jnp.max(dmat, axis=-1, keepdims=True))
        w_intra = jnp.exp(dmat - m_t)
        w_inter = jnp.exp(g - m_t)
        qh = q_ref[:, cols]
        kh = k_ref[:, cols]
        vh = v_ref[:, cols]
        s = lax.dot_general(qh, kh, (((1,), (1,)), ((), ())), preferred_element_type=F32) * scale * w_intra
        c_old = c_scr[h]
        n_old = n_scr[h:h + 1, :]
        num = (jnp.dot(s.astype(BF16), vh, preferred_element_type=F32)
               + w_inter * jnp.dot(qh, c_old.astype(BF16), preferred_element_type=F32))
        den = (jnp.sum(s, axis=-1, keepdims=True)
               + w_inter * jnp.sum(qh.astype(F32) * n_old, axis=-1, keepdims=True))
        out = num / jnp.maximum(jnp.abs(den), jnp.exp(-m_t))

        b_last = bc[L - 1:L, :]
        dec = b_last - bc + lic
        m_new = jnp.maximum(b_last + m_prev, jnp.max(dec, axis=0, keepdims=True))
        ws = jnp.exp(dec - m_new)
        wc = jnp.exp(b_last + m_prev - m_new)
        kw = kh.astype(F32) * (ws * scale)
        c_scr[h] = wc * c_old + lax.dot_general(kw.astype(BF16), vh, (((0,), (0,)), ((), ())),
                                                preferred_element_type=F32)
        n_scr[h:h + 1, :] = wc * n_old + jnp.sum(kw, axis=0, keepdims=True)
        m_scr[h:h + 1, :] = jnp.broadcast_to(m_new, (1, m_scr.shape[1]))

        mu = jnp.mean(out, axis=-1, keepdims=True)
        dlt = out - mu
        var = jnp.mean(dlt * dlt, axis=-1, keepdims=True)
        hn = dlt * lax.rsqrt(var + LN_EPS) * ng_ref[:, cols]
        hn = hn + skip_ref[:, cols] * xc_ref[:, cols].astype(F32)
        o_ref[:, cols] = (_sigmoid(z_ref[:, cols].astype(F32)) * hn).astype(BF16)


def ml_chunk(q, k, v, xc, zz, gates, gates_t, norm_g, skip, bsz, seq):
    t, width = q.shape
    hd = width // ML_HEADS
    L = ML_L
    cpb = seq // L
    row = lambda b, c: (b * cpb + c, 0)
    act = lambda: pl.BlockSpec((L, width), row)
    return pl.pallas_call(
        _ml_chunk_kernel,
        grid=(bsz, cpb),
        in_specs=[
            act(), act(), act(), act(),
            pl.BlockSpec((L, width), lambda b, c: (b * cpb + c, 1)),
            pl.BlockSpec((L, 2 * ML_HEADS), row),
            pl.BlockSpec((2 * ML_HEADS, L), lambda b, c: (0, b * cpb + c)),
            _const_spec((1, width)), _const_spec((1, width)),
        ],
        out_specs=act(),
        out_shape=jax.ShapeDtypeStruct((t, width), BF16),
        scratch_shapes=[pltpu.VMEM((ML_HEADS, hd, hd), F32), pltpu.VMEM((ML_HEADS, hd), F32),
                        pltpu.VMEM((ML_HEADS, 128), F32)],
        compiler_params=_cparams(("parallel", "arbitrary")),
        name="ml_chunk",
    )(q, k, v, xc, zz, gates, gates_t, norm_g.reshape(1, width), skip.reshape(1, width))


def _router_kernel(h_ref, w_ref, b_ref, eid_ref, prob_ref, rank_ref, cnt_ref, cnt_scr):
    tr = h_ref.shape[0]
    ne = w_ref.shape[0]

    @pl.when(pl.program_id(0) == 0)
    def _():
        cnt_scr[...] = jnp.zeros_like(cnt_scr)

    logits = lax.dot_general(w_ref[...], _unpack_rows(h_ref[...]), (((1,), (1,)), ((), ())),
                             preferred_element_type=F32) + b_ref[...]
    sub = lax.broadcasted_iota(jnp.int32, (ne, tr), 0)
    work = logits
    sel = jnp.zeros((ne, tr), F32)
    hits = []
    top = None
    for k in range(TOP_K):
        m = jnp.max(work, axis=0, keepdims=True)
        idx = jnp.min(jnp.where(work == m, sub, ne), axis=0, keepdims=True)
        hit = sub == idx
        hits.append((hit, idx))
        sel = jnp.where(hit, 1.0, sel)
        work = jnp.where(hit, -jnp.inf, work)
        if k == 0:
            top = m
    e = sel * jnp.exp(logits - top)
    p = e / jnp.sum(e, axis=0, keepdims=True)

    ri = lax.broadcasted_iota(jnp.int32, (tr, tr), 0)
    ci = lax.broadcasted_iota(jnp.int32, (tr, tr), 1)
    strict_upper = (ri < ci).astype(BF16)
    before = jnp.dot(sel.astype(BF16), strict_upper, preferred_element_type=F32)
    rank = cnt_scr[...] + before
    cnt_scr[...] = cnt_scr[...] + jnp.sum(sel, axis=1, keepdims=True)
    cnt_ref[...] = cnt_scr[...].astype(jnp.int32)

    row = lax.broadcasted_iota(jnp.int32, (SUBLANES, tr), 0)
    eid = jnp.zeros((SUBLANES, tr), jnp.int32)
    prob = jnp.zeros((SUBLANES, tr), F32)
    rnk = jnp.zeros((SUBLANES, tr), F32)
    for k, (hit, idx) in enumerate(hits):
        slot = row == k
        eid = jnp.where(slot, idx, eid)
        prob = jnp.where(slot, jnp.sum(jnp.where(hit, p, 0.0), axis=0, keepdims=True), prob)
        rnk = jnp.where(slot, jnp.sum(jnp.where(hit, rank, 0.0), axis=0, keepdims=True), rnk)
    eid_ref[...] = eid
    prob_ref[...] = prob
    rank_ref[...] = rnk.astype(jnp.int32)


def router(h2, w_router, b_router, tr=512):
    t, dp = h2.shape
    d, ne = w_router.shape
    tr = min(tr, t)
    col = lambda i: (0, i)
    return pl.pallas_call(
        _router_kernel,
        grid=(t // tr,),
        in_specs=[pl.BlockSpec((tr, dp), lambda i: (i, 0)), _const_spec((ne, d)), _const_spec((ne, 1))],
        out_specs=[pl.BlockSpec((SUBLANES, tr), col), pl.BlockSpec((SUBLANES, tr), col),
                   pl.BlockSpec((SUBLANES, tr), col), pl.BlockSpec((ne, 1), lambda i: (0, 0))],
        out_shape=[jax.ShapeDtypeStruct((SUBLANES, t), jnp.int32), jax.ShapeDtypeStruct((SUBLANES, t), F32),
                   jax.ShapeDtypeStruct((SUBLANES, t), jnp.int32), jax.ShapeDtypeStruct((ne, 1), jnp.int32)],
        scratch_shapes=[pltpu.VMEM((ne, 1), F32)],
        compiler_params=_cparams(("arbitrary",)),
        name="router",
    )(h2, w_router.T.astype(BF16), b_router.reshape(ne, 1))


def _expert_kernel(te_ref, nu_ref, nx_ref, x_ref, wgu_hbm, bgu_ref, wd_hbm, bd_ref, o_ref,
                   wgu_f32, wd_f32, wgu_bf, wd_bf, sem, *, layer):
    i = pl.program_id(0)
    used = i < nu_ref[0]
    e = te_ref[i]
    new_expert = jnp.logical_or(i == 0, e != te_ref[jnp.maximum(i - 1, 0)])

    def fetch(expert):
        return (pltpu.make_async_copy(wgu_hbm.at[layer, expert], wgu_f32, sem.at[0]),
                pltpu.make_async_copy(wd_hbm.at[layer, expert], wd_f32, sem.at[1]))

    @pl.when(jnp.logical_and(used, i == 0))
    def _():
        for cp in fetch(e):
            cp.start()

    @pl.when(jnp.logical_and(used, new_expert))
    def _():
        for cp in fetch(e):
            cp.wait()
        wgu_bf[...] = wgu_f32[...].astype(BF16)
        wd_bf[...] = wd_f32[...].astype(BF16)
        nxt = nx_ref[i]

        @pl.when(nxt != e)
        def _():
            for cp in fetch(nxt):
                cp.start()

    @pl.when(used)
    def _():
        f = wd_bf.shape[0]
        gu = jnp.dot(_unpack_rows(x_ref[...]), wgu_bf[...], preferred_element_type=F32) + bgu_ref[...]
        gate = jnp.minimum(gu[:, :f], SWIGLU_LIMIT)
        up = jnp.clip(gu[:, f:], -SWIGLU_LIMIT, SWIGLU_LIMIT)
        act = gate * _sigmoid(SWIGLU_ALPHA * gate) * (up + 1.0)
        y = jnp.dot(act.astype(BF16), wd_bf[...], preferred_element_type=F32) + bd_ref[...]
        o_ref[...] = _pack_rows(y)

    @pl.when(jnp.logical_not(used))
    def _():
        o_ref[...] = jnp.zeros_like(o_ref)


def experts(xs, tile_e, n_used, tile_next, w_gu, b_gu, w_d, b_d, layer, deps=()):
    p, dp = xs.shape
    depth, ne, d, f2 = w_gu.shape
    f = f2 // 2
    te = EXPERT_TILE
    grid_spec = pltpu.PrefetchScalarGridSpec(
        num_scalar_prefetch=3,
        grid=(p // te,),
        in_specs=[
            pl.BlockSpec((te, dp), lambda i, e, n, nx: (jnp.minimum(i, jnp.maximum(n[0] - 1, 0)), 0)),
            pl.BlockSpec(memory_space=pl.ANY),
            pl.BlockSpec((None, None, 1, f2), lambda i, e, n, nx: (layer, e[i], 0, 0)),
            pl.BlockSpec(memory_space=pl.ANY),
            pl.BlockSpec((None, None, 1, d), lambda i, e, n, nx: (layer, e[i], 0, 0)),
        ] + _dep_specs(deps),
        out_specs=pl.BlockSpec((te, dp), lambda i, e, n, nx: (i, 0)),
        scratch_shapes=[pltpu.VMEM((d, f2), F32), pltpu.VMEM((f, d), F32),
                        pltpu.VMEM((d, f2), BF16), pltpu.VMEM((f, d), BF16),
                        pltpu.SemaphoreType.DMA((2,))],
    )
    return pl.pallas_call(
        _after(functools.partial(_expert_kernel, layer=layer), 8, deps),
        grid_spec=grid_spec,
        out_shape=jax.ShapeDtypeStruct((p, dp), jnp.uint32),
        compiler_params=_cparams(("arbitrary",)),
        name="experts",
    )(tile_e, n_used, tile_next, xs, w_gu, b_gu.reshape(depth, ne, 1, f2), w_d,
      b_d.reshape(depth, ne, 1, d), *deps)


def _combine_ln_kernel(y0_ref, y1_ref, y2_ref, y3_ref, p_ref, x_ref, g_ref, lng_ref, lnb_ref, xo_ref,
                       *, alpha):
    p = p_ref[...]
    y = None
    for k, y_ref in enumerate((y0_ref, y1_ref, y2_ref, y3_ref)):
        term = p[:, k:k + 1] * _unpack_rows(y_ref[...]).astype(F32)
        y = term if y is None else y + term
    v = alpha * x_ref[...] + (1.0 + g_ref[...]) * y
    xo_ref[...] = _layer_norm_rows(v, lng_ref[...], lnb_ref[...])


def combine_ln(ys, prob, x, mod4, ln_g, ln_b, seq, alpha, deps=(), tm=256):
    t, d = x.shape
    ne = prob.shape[1]
    rows_per_b = seq // tm
    row = lambda i: (i, 0)
    return pl.pallas_call(
        _after(functools.partial(_combine_ln_kernel, alpha=alpha), 9, deps),
        grid=(t // tm,),
        in_specs=[pl.BlockSpec((tm, d // 2), row)] * 4 + [
            pl.BlockSpec((tm, ne), row),
            pl.BlockSpec((tm, d), row),
            pl.BlockSpec((None, None, 1, d), lambda i: (i // rows_per_b, 5, 0, 0)),
            _const_spec((1, d)), _const_spec((1, d)),
        ] + _dep_specs(deps),
        out_specs=pl.BlockSpec((tm, d), row),
        out_shape=jax.ShapeDtypeStruct((t, d), F32),
        compiler_params=_cparams(("parallel",)),
        name="combine_ln",
    )(*ys, prob, x, mod4, ln_g.reshape(1, d), ln_b.reshape(1, d), *deps)


def moe_route(h2, w_router, b_router):
    t, d = h2.shape
    ne = w_router.shape[1]
    te = EXPERT_TILE
    eid, prob, rank, counts2d = router(h2, w_router, b_router)
    counts = counts2d[:, 0]
    padded = ((counts + te - 1) // te) * te
    ends = jnp.cumsum(padded)
    starts = ends - padded
    prob = prob.T
    eid4 = eid[:TOP_K]
    first_slot = jnp.zeros_like(eid4)
    for e in range(ne):
        first_slot = jnp.where(eid4 == e, starts[e], first_slot)
    dest4 = first_slot + rank[:TOP_K]
    n_slots = t * TOP_K + ne * te
    tok = jnp.broadcast_to(jnp.arange(t, dtype=jnp.int32)[None, :], (TOP_K, t))
    filler = jnp.arange(n_slots, dtype=jnp.int32) % t
    src = filler.at[dest4.reshape(-1)].set(tok.reshape(-1), unique_indices=True, mode=_IN_BOUNDS)
    n_tiles = n_slots // te
    tile_start = jnp.arange(n_tiles, dtype=jnp.int32) * te
    tile_e = jnp.sum((ends[None, :] <= tile_start[:, None]).astype(jnp.int32), axis=1)
    n_used = (ends[-1] // te).astype(jnp.int32).reshape(1)
    last_e = jnp.sum((ends <= ends[-1] - 1).astype(jnp.int32))
    tile_e = jnp.where(tile_start < ends[-1], tile_e, last_e).astype(jnp.int32)
    ids = jnp.arange(ne, dtype=jnp.int32)
    later = jnp.where((ids[None, :] > ids[:, None]) & (padded[None, :] > 0), ids[None, :], ne)
    nxt = jnp.min(later, axis=1)
    nxt = jnp.where(nxt == ne, ids, nxt)
    tile_next = jnp.sum(jnp.where(tile_e[:, None] == ids[None, :], nxt[None, :], 0), axis=1).astype(jnp.int32)
    return dict(src=src, dest4=dest4, prob=prob, tile_e=tile_e, n_used=n_used, tile_next=tile_next,
                order_token=counts2d)


def kernel(x, c, ada_w, ada_b, ln1_g, ln1_b, ln2_g, ln2_b, rg_w_in, rg_conv_w, rg_conv_b, rg_w_rgate, rg_b_rgate, rg_w_igate, rg_b_igate, rg_lam, rg_w_out, sg_w_in, sg_b_in, sg_ln_g, sg_ln_b, sg_w_sp, sg_b_sp, sg_w_out, sg_b_out, ml_w_in, ml_conv_w, ml_conv_b, ml_w_q, ml_w_k, ml_w_v, ml_w_gates, ml_b_gates, ml_skip, ml_norm_g, ml_w_out, moe_w_router, moe_b_router, moe_w_gate_up, moe_b_gate_up, moe_w_down, moe_b_down):
    bsz, seq, d = x.shape
    depth = ada_w.shape[0]
    alpha = (2 * depth) ** 0.25
    xfull = x.reshape(bsz * seq, d)
    mod = ada_mod(c, ada_w, ada_b)
    zero_bias = jnp.zeros((d,), F32)

    groups = BATCH_GROUPS if bsz % BATCH_GROUPS == 0 else 1
    gb = bsz // groups
    xs = [xfull] * groups
    offs = [g * gb * seq for g in range(groups)]

    for i in range(depth):
        kind, j = i % 3, i // 3
        if kind == 0:
            w_in = rg_w_in[j].astype(BF16)
            w_ri = _rg_gate_weights(rg_w_rgate[j], rg_w_igate[j])
            w_out, b_out = rg_w_out[j].astype(BF16), zero_bias
        elif kind == 1:
            w_in = sg_w_in[j].astype(BF16)
            w_out, b_out = sg_w_out[j].astype(BF16), sg_b_out[j]
        else:
            w_in = ml_w_in[j].astype(BF16)
            wq, wk, wv = (_ml_headwise_weights(w[j]) for w in (ml_w_q, ml_w_k, ml_w_v))
            w_out, b_out = ml_w_out[j].astype(BF16), zero_bias
        mods, routes, x_mid, h_mid = [], [], [], []
        prev = ()
        for g in range(groups):
            mod4 = mod[i, g * gb:(g + 1) * gb].reshape(gb, 6, 1, d)
            xt, off = xs[g], offs[g]
            if kind == 0:
                z = mm_in(xt, mod4, 1, 0, w_in, seq, off, deps=prev)
                a = rg_core(z, rg_conv_w[j], rg_conv_b[j], w_ri, rg_b_rgate[j], rg_b_igate[j], rg_lam[j],
                            gb, seq)
            elif kind == 1:
                u, v = sg_in(xt, mod4, w_in, sg_b_in[j], sg_ln_g[j], sg_ln_b[j], seq, deps=prev)
                a = sg_mix(u, v, sg_w_sp[j], sg_b_sp[j])
            else:
                zz = mm_in(xt, mod4, 1, 0, w_in, seq, off, deps=prev)
                q, k, v, xc, gates = ml_qkv(zz, ml_conv_w[j], ml_conv_b[j], wq, wk, wv,
                                            ml_w_gates[j], ml_b_gates[j], gb, seq)
                a = ml_chunk(q, k, v, xc, zz, gates, gates.T, ml_norm_g[j], ml_skip[j], gb, seq)
            xt, h2 = mm_res_ln(a, w_out, b_out, xt, mod4, 2, 4, 3, ln1_g[i], ln1_b[i], seq, alpha, off)
            route = moe_route(h2, moe_w_router[i], moe_b_router[i])
            prev = (route["order_token"],)
            mods.append(mod4); routes.append(route); x_mid.append(xt); h_mid.append(h2)
        ysorted = []
        for g in range(groups):
            r = routes[g]
            rows = h_mid[g].at[r["src"]].get(mode=_IN_BOUNDS)
            ysorted.append(experts(rows, r["tile_e"], r["n_used"], r["tile_next"], moe_w_gate_up,
                                   moe_b_gate_up, moe_w_down, moe_b_down, i, deps=prev))
            prev = (ysorted[g],)
        for g in range(groups):
            r = routes[g]
            ys = [ysorted[g].at[r["dest4"][k]].get(mode=_IN_BOUNDS, unique_indices=True)
                  for k in range(TOP_K)]
            xs[g] = combine_ln(ys, r["prob"], x_mid[g], mods[g], ln2_g[i], ln2_b[i], seq, alpha, deps=prev)
            prev = (xs[g],)
            offs[g] = 0
    return jnp.concatenate(xs, axis=0).reshape(bsz, seq, d)
```

```python
import functools
import math

import jax
import jax.numpy as jnp
from jax import lax
from jax.experimental import pallas as pl
from jax.experimental.pallas import tpu as pltpu

F32 = jnp.float32
BF16 = jnp.bfloat16

LN_EPS = 1e-5
RG_C = 8.0
CONV_W = 4
CHUNK = 64
SG_BLOCK = 128
SG_GROUPS = 8
ML_HEADS = 8
TOP_K = 4
SWIGLU_LIMIT = 7.0
SWIGLU_ALPHA = 1.702

SUBLANES = 8
RG_SUPER = 640
ML_L = 256
EXPERT_TILE = 512
BATCH_GROUPS = 1
VMEM_LIMIT = 56 * 1024 * 1024
NEG = -1e30
_IN_BOUNDS = "promise_in_bounds"


def _cparams(sem):
    return pltpu.CompilerParams(dimension_semantics=sem, vmem_limit_bytes=VMEM_LIMIT)


def _sigmoid(x):
    return 1.0 / (1.0 + jnp.exp(-x))


def _gelu_tanh(x):
    return 0.5 * x * (1.0 + jnp.tanh(math.sqrt(2.0 / math.pi) * (x + 0.044715 * (x * x * x))))


def _softplus(x):
    return jnp.maximum(x, 0.0) + jnp.log1p(jnp.exp(-jnp.abs(x)))


def _layer_norm_rows(v, g, b):
    mu = jnp.mean(v, axis=-1, keepdims=True)
    d = v - mu
    var = jnp.mean(d * d, axis=-1, keepdims=True)
    return d * lax.rsqrt(var + LN_EPS) * g + b


def _pack_rows(v):
    n = v.shape[1] // 2
    lo = lax.bitcast_convert_type(v[:, :n].astype(BF16).astype(F32), jnp.uint32)
    hi = lax.bitcast_convert_type(v[:, n:].astype(BF16).astype(F32), jnp.uint32)
    return (lo >> 16) | hi


def _unpack_rows(w):
    lo = lax.bitcast_convert_type(w << 16, F32)
    hi = lax.bitcast_convert_type(w & jnp.uint32(0xFFFF0000), F32)
    return jnp.concatenate([lo.astype(BF16), hi.astype(BF16)], axis=1)


def _const_spec(shape):
    nd = len(shape)
    return pl.BlockSpec(shape, lambda *_: (0,) * nd, pipeline_mode=pl.Buffered(1))


def _after(kernel_fn, n_in, deps):
    nd = len(deps)
    if nd == 0:
        return kernel_fn

    def body(*refs):
        return kernel_fn(*refs[:n_in], *refs[n_in + nd:])

    return body


def _dep_specs(deps):
    return [pl.BlockSpec(memory_space=pl.ANY)] * len(deps)


def _ada_kernel(c_ref, w_ref, b_ref, o_ref):
    c = c_ref[...]
    cond = c * _sigmoid(c)
    o_ref[...] = jnp.dot(cond.astype(BF16), w_ref[...].astype(BF16),
                         preferred_element_type=F32) + b_ref[...]


def ada_mod(c, ada_w, ada_b, tn=1024):
    depth, d, n = ada_w.shape
    bsz = c.shape[0]
    return pl.pallas_call(
        _ada_kernel,
        grid=(depth, n // tn),
        in_specs=[
            pl.BlockSpec((bsz, d), lambda l, j: (0, 0)),
            pl.BlockSpec((None, d, tn), lambda l, j: (l, 0, j)),
            pl.BlockSpec((None, 1, tn), lambda l, j: (l, 0, j)),
        ],
        out_specs=pl.BlockSpec((None, bsz, tn), lambda l, j: (l, 0, j)),
        out_shape=jax.ShapeDtypeStruct((depth, bsz, n), F32),
        compiler_params=_cparams(("parallel", "parallel")),
        name="ada_mod",
    )(c, ada_w, ada_b.reshape(depth, 1, n))


def _mm_in_kernel(x_ref, sc_ref, sh_ref, w_ref, o_ref, *, tn):
    h = (x_ref[...] * (1.0 + sc_ref[...]) + sh_ref[...]).astype(BF16)
    for j in range(o_ref.shape[1] // tn):
        cols = slice(j * tn, (j + 1) * tn)
        o_ref[:, cols] = jnp.dot(h, w_ref[:, cols], preferred_element_type=F32).astype(o_ref.dtype)


def mm_in(x, mod4, k_sc, k_sh, w, seq, row_off=0, deps=(), tm=512, tn=1024):
    t, d = mod4.shape[0] * seq, x.shape[1]
    n = w.shape[1]
    tm = min(tm, seq)
    rows_per_b = seq // tm
    blk_off = row_off // tm
    return pl.pallas_call(
        _after(functools.partial(_mm_in_kernel, tn=tn), 4, deps),
        grid=(t // tm,),
        in_specs=[
            pl.BlockSpec((tm, d), lambda i: (i + blk_off, 0)),
            pl.BlockSpec((None, None, 1, d), lambda i: (i // rows_per_b, k_sc, 0, 0)),
            pl.BlockSpec((None, None, 1, d), lambda i: (i // rows_per_b, k_sh, 0, 0)),
            _const_spec((d, n)),
        ] + _dep_specs(deps),
        out_specs=pl.BlockSpec((tm, n), lambda i: (i, 0)),
        out_shape=jax.ShapeDtypeStruct((t, n), BF16),
        compiler_params=_cparams(("parallel",)),
        name="mm_in",
    )(x, mod4, mod4, w, *deps)


def _mm_res_ln_kernel(a_ref, w_ref, b_ref, x_ref, g_ref, lng_ref, lnb_ref, sc_ref, sh_ref,
                      xo_ref, ho_ref, *, alpha):
    y = jnp.dot(a_ref[...], w_ref[...], preferred_element_type=F32) + b_ref[...]
    v = alpha * x_ref[...] + (1.0 + g_ref[...]) * y
    xn = _layer_norm_rows(v, lng_ref[...], lnb_ref[...])
    xo_ref[...] = xn
    ho_ref[...] = _pack_rows(xn * (1.0 + sc_ref[...]) + sh_ref[...])


def mm_res_ln(a, w, bias, x, mod4, k_gate, k_sc, k_sh, ln_g, ln_b, seq, alpha, row_off=0, tm=512):
    t, k = a.shape
    d = x.shape[1]
    rows_per_b = seq // tm
    blk_off = row_off // tm
    modspec = lambda kk: pl.BlockSpec((None, None, 1, d), lambda i: (i // rows_per_b, kk, 0, 0))
    return pl.pallas_call(
        functools.partial(_mm_res_ln_kernel, alpha=alpha),
        grid=(t // tm,),
        in_specs=[
            pl.BlockSpec((tm, k), lambda i: (i, 0)),
            _const_spec((k, d)),
            _const_spec((1, d)),
            pl.BlockSpec((tm, d), lambda i: (i + blk_off, 0)),
            modspec(k_gate),
            _const_spec((1, d)),
            _const_spec((1, d)),
            modspec(k_sc),
            modspec(k_sh),
        ],
        out_specs=[pl.BlockSpec((tm, d), lambda i: (i, 0)), pl.BlockSpec((tm, d // 2), lambda i: (i, 0))],
        out_shape=[jax.ShapeDtypeStruct((t, d), F32), jax.ShapeDtypeStruct((t, d // 2), jnp.uint32)],
        compiler_params=_cparams(("parallel",)),
        name="mm_res_ln",
    )(a, w, bias.reshape(1, d), x, mod4, ln_g.reshape(1, d), ln_b.reshape(1, d), mod4, mod4)


def _causal_conv(buf_ref, x, w_ref, b_ref, first):
    ts = x.shape[0]

    @pl.when(first)
    def _():
        buf_ref[0:SUBLANES, :] = jnp.zeros((SUBLANES, x.shape[1]), F32)

    buf_ref[SUBLANES:SUBLANES + ts, :] = x
    y = b_ref[...] + w_ref[CONV_W - 1:CONV_W, :] * x
    for back in range(1, CONV_W):
        y = y + w_ref[CONV_W - 1 - back:CONV_W - back, :] * buf_ref[SUBLANES - back:SUBLANES - back + ts, :]
    buf_ref[0:SUBLANES, :] = buf_ref[ts:ts + SUBLANES, :]
    return y


def _scan_rows(a, u, h0):
    ts, c = a.shape
    groups = ts // SUBLANES
    a3 = a.reshape(groups, SUBLANES, c)
    u3 = u.reshape(groups, SUBLANES, c)
    rid = lax.broadcasted_iota(jnp.int32, a3.shape, 1)
    for k in (1, 2, 4):
        keep = rid >= k
        a_sh = jnp.where(keep, pltpu.roll(a3, k, 1), 1.0)
        u_sh = jnp.where(keep, pltpu.roll(u3, k, 1), 0.0)
        u3 = u3 + a3 * u_sh
        a3 = a3 * a_sh
    h = h0
    outs = []
    for j in range(groups):
        hj = u3[j] + a3[j] * h
        outs.append(hj)
        h = hj[SUBLANES - 1:SUBLANES, :]
    return jnp.concatenate(outs, axis=0), h


def _rg_core_kernel(gate_ref, rec_ref, cw_ref, cb_ref, wri_ref, br_ref, bi_ref, lam_ref,
                    o_ref, buf_ref, h_ref):
    first = pl.program_id(2) == 0
    c = o_ref.shape[1]
    xr = _causal_conv(buf_ref, rec_ref[...].astype(F32), cw_ref, cb_ref, first)
    pre = jnp.dot(xr.astype(BF16), wri_ref[...], preferred_element_type=F32)
    r = _sigmoid(pre[:, :c] + br_ref[...])
    i = _sigmoid(pre[:, c:] + bi_ref[...])
    log_a = (-RG_C) * r * _softplus(-lam_ref[...])
    a = jnp.exp(log_a)
    u = jnp.sqrt(-jnp.tanh(log_a) * (a * a + 1.0)) * (i * xr)

    @pl.when(first)
    def _():
        h_ref[...] = jnp.zeros_like(h_ref)

    hs, h_last = _scan_rows(a, u, h_ref[...])
    h_ref[...] = h_last
    o_ref[...] = (_gelu_tanh(gate_ref[...].astype(F32)) * hs).astype(BF16)


def rg_core(z, conv_w, conv_b, w_ri, b_r, b_i, lam, bsz, seq, ts=512):
    t = z.shape[0]
    width = conv_w.shape[1]
    nsup = width // RG_SUPER
    spb = seq // ts
    row = lambda c, b, s: b * spb + s
    vec = lambda: pl.BlockSpec((1, RG_SUPER), lambda c, b, s: (0, c))
    return pl.pallas_call(
        _rg_core_kernel,
        grid=(nsup, bsz, spb),
        in_specs=[
            pl.BlockSpec((ts, RG_SUPER), lambda c, b, s: (row(c, b, s), c)),
            pl.BlockSpec((ts, RG_SUPER), lambda c, b, s: (row(c, b, s), nsup + c)),
            pl.BlockSpec((CONV_W, RG_SUPER), lambda c, b, s: (0, c)),
            vec(),
            pl.BlockSpec((None, RG_SUPER, 2 * RG_SUPER), lambda c, b, s: (c, 0, 0)),
            vec(), vec(), vec(),
        ],
        out_specs=pl.BlockSpec((ts, RG_SUPER), lambda c, b, s: (row(c, b, s), c)),
        out_shape=jax.ShapeDtypeStruct((t, width), BF16),
        scratch_shapes=[pltpu.VMEM((ts + SUBLANES, RG_SUPER), F32), pltpu.VMEM((1, RG_SUPER), F32)],
        compiler_params=_cparams(("parallel", "parallel", "arbitrary")),
        name="rg_core",
    )(z, z, conv_w, conv_b.reshape(1, width), w_ri, b_r.reshape(1, width), b_i.reshape(1, width),
      lam.reshape(1, width))


def _rg_gate_weights(w_r, w_i):
    nb, bs, _ = w_r.shape
    per = RG_SUPER // bs
    nsup = nb // per

    def dense(w):
        w4 = w.reshape(nsup, per, bs, bs)
        eye = jnp.eye(per, dtype=w.dtype)
        return jnp.einsum("spcd,pq->spcqd", w4, eye).reshape(nsup, RG_SUPER, RG_SUPER)

    return jnp.concatenate([dense(w_r), dense(w_i)], axis=-1).astype(BF16)


def _sg_in_kernel(x_ref, sc_ref, sh_ref, w_ref, b_ref, lng_ref, lnb_ref, u_ref, v_ref):
    half = u_ref.shape[1]
    h = (x_ref[...] * (1.0 + sc_ref[...]) + sh_ref[...]).astype(BF16)
    zu = jnp.dot(h, w_ref[:, :half], preferred_element_type=F32) + b_ref[:, :half]
    u_ref[...] = _gelu_tanh(zu).astype(BF16)
    zv = jnp.dot(h, w_ref[:, half:], preferred_element_type=F32) + b_ref[:, half:]
    v_ref[...] = _layer_norm_rows(_gelu_tanh(zv), lng_ref[...], lnb_ref[...]).astype(BF16)


def sg_in(x, mod4, w, b, ln_g, ln_b, seq, deps=(), tm=512):
    t, d = x.shape
    half = w.shape[1] // 2
    rows_per_b = seq // tm
    modspec = lambda kk: pl.BlockSpec((None, None, 1, d), lambda i: (i // rows_per_b, kk, 0, 0))
    row = lambda i: (i, 0)
    return pl.pallas_call(
        _after(_sg_in_kernel, 7, deps),
        grid=(t // tm,),
        in_specs=[
            pl.BlockSpec((tm, d), row),
            modspec(1), modspec(0),
            _const_spec((d, 2 * half)),
            _const_spec((1, 2 * half)),
            _const_spec((1, half)), _const_spec((1, half)),
        ] + _dep_specs(deps),
        out_specs=[pl.BlockSpec((tm, half), row), pl.BlockSpec((tm, half), row)],
        out_shape=[jax.ShapeDtypeStruct((t, half), BF16), jax.ShapeDtypeStruct((t, half), BF16)],
        compiler_params=_cparams(("parallel",)),
        name="sg_in",
    )(x, mod4, mod4, w, b.reshape(1, 2 * half), ln_g.reshape(1, half), ln_b.reshape(1, half), *deps)


def _sg_mix_kernel(u_ref, v_ref, w_ref, bt_ref, o_ref):
    nblk = u_ref.shape[0] // SG_BLOCK
    gw = u_ref.shape[1] // SG_GROUPS
    ii = lax.broadcasted_iota(jnp.int32, (SG_BLOCK, SG_BLOCK), 0)
    jj = lax.broadcasted_iota(jnp.int32, (SG_BLOCK, SG_BLOCK), 1)
    allowed = (jj // CHUNK) <= (ii // CHUNK)
    for g in range(SG_GROUPS):
        w = jnp.where(allowed, w_ref[g], 0.0).astype(BF16)
        bias = bt_ref[:, g:g + 1]
        for n in range(nblk):
            rows = slice(n * SG_BLOCK, (n + 1) * SG_BLOCK)
            cols = slice(g * gw, (g + 1) * gw)
            mixed = jnp.dot(w, v_ref[rows, cols], preferred_element_type=F32) + bias
            o_ref[rows, cols] = (u_ref[rows, cols].astype(F32) * mixed).astype(BF16)


def sg_mix(u, v, w_sp, b_sp, tm=256):
    t, half = u.shape
    return pl.pallas_call(
        _sg_mix_kernel,
        grid=(t // tm,),
        in_specs=[
            pl.BlockSpec((tm, half), lambda i: (i, 0)),
            pl.BlockSpec((tm, half), lambda i: (i, 0)),
            _const_spec(w_sp.shape),
            _const_spec((SG_BLOCK, SG_GROUPS)),
        ],
        out_specs=pl.BlockSpec((tm, half), lambda i: (i, 0)),
        out_shape=jax.ShapeDtypeStruct((t, half), BF16),
        compiler_params=_cparams(("parallel",)),
        name="sg_mix",
    )(u, v, w_sp, b_sp.T)


def _ml_qkv_kernel(xm_ref, cw_ref, cb_ref, wq_ref, wk_ref, wv_ref, wg_ref, bg_ref,
                   q_ref, k_ref, v_ref, xc_ref, g_ref, buf_ref):
    first = pl.program_id(1) == 0
    width = xm_ref.shape[1]
    hd = width // ML_HEADS
    xm = xm_ref[...]
    conv = _causal_conv(buf_ref, xm.astype(F32), cw_ref, cb_ref, first)
    xc = (conv * _sigmoid(conv)).astype(BF16)
    xc_ref[...] = xc
    for h in range(ML_HEADS):
        cols = slice(h * hd, (h + 1) * hd)
        q_ref[:, cols] = jnp.dot(xc[:, cols], wq_ref[h], preferred_element_type=F32).astype(BF16)
        k_ref[:, cols] = jnp.dot(xc[:, cols], wk_ref[h], preferred_element_type=F32).astype(BF16)
        v_ref[:, cols] = jnp.dot(xm[:, cols], wv_ref[h], preferred_element_type=F32).astype(BF16)
    gates = (jnp.dot(q_ref[...], wg_ref[0], preferred_element_type=F32)
             + jnp.dot(k_ref[...], wg_ref[1], preferred_element_type=F32)
             + jnp.dot(v_ref[...], wg_ref[2], preferred_element_type=F32) + bg_ref[...])
    is_f = lax.broadcasted_iota(jnp.int32, gates.shape, 1) >= ML_HEADS
    g_ref[...] = jnp.where(is_f, -_softplus(-gates), gates)


def _ml_headwise_weights(w):
    nqb, bs, _ = w.shape
    per = nqb // ML_HEADS
    w4 = w.reshape(ML_HEADS, per, bs, bs)
    eye = jnp.eye(per, dtype=w.dtype)
    return jnp.einsum("hpcd,pq->hpcqd", w4, eye).reshape(ML_HEADS, per * bs, per * bs).astype(BF16)


def ml_qkv(zz, conv_w, conv_b, wq, wk, wv, w_gates, b_gates, bsz, seq, ts=256):
    t = zz.shape[0]
    width = conv_w.shape[1]
    hd = width // ML_HEADS
    spb = seq // ts
    row = lambda b, s: (b * spb + s, 0)
    act = lambda: pl.BlockSpec((ts, width), row)
    return pl.pallas_call(
        _ml_qkv_kernel,
        grid=(bsz, spb),
        in_specs=[
            act(),
            _const_spec((CONV_W, width)), _const_spec((1, width)),
            _const_spec((ML_HEADS, hd, hd)), _const_spec((ML_HEADS, hd, hd)), _const_spec((ML_HEADS, hd, hd)),
            _const_spec((3, width, 2 * ML_HEADS)), _const_spec((1, 2 * ML_HEADS)),
        ],
        out_specs=[act(), act(), act(), act(), pl.BlockSpec((ts, 2 * ML_HEADS), row)],
        out_shape=[jax.ShapeDtypeStruct((t, width), BF16)] * 4
                  + [jax.ShapeDtypeStruct((t, 2 * ML_HEADS), F32)],
        scratch_shapes=[pltpu.VMEM((ts + SUBLANES, width), F32)],
        compiler_params=_cparams(("parallel", "arbitrary")),
        name="ml_qkv",
    )(zz, conv_w, conv_b.reshape(1, width), wq, wk, wv,
      w_gates.reshape(3, width, 2 * ML_HEADS).astype(BF16), b_gates.reshape(1, 2 * ML_HEADS))


def _ml_chunk_kernel(q_ref, k_ref, v_ref, xc_ref, z_ref, gc_ref, gr_ref, ng_ref, skip_ref,
                     o_ref, c_scr, n_scr, m_scr):
    L, width = q_ref.shape
    hd = width // ML_HEADS
    scale = hd ** -0.5

    @pl.when(pl.program_id(1) == 0)
    def _():
        c_scr[...] = jnp.zeros_like(c_scr)
        n_scr[...] = jnp.zeros_like(n_scr)
        m_scr[...] = jnp.zeros_like(m_scr)

    ri = lax.broadcasted_iota(jnp.int32, (L, L), 0)
    ci = lax.broadcasted_iota(jnp.int32, (L, L), 1)
    causal = ri >= ci
    lower = causal.astype(F32)
    upper = (ri <= ci).astype(F32)
    gc = gc_ref[...]
    gr = gr_ref[...]
    bcol = jnp.dot(lower, gc[:, ML_HEADS:], preferred_element_type=F32, precision=lax.Precision.HIGHEST)
    brow = jnp.dot(gr[ML_HEADS:, :], upper, preferred_element_type=F32, precision=lax.Precision.HIGHEST)

    for h in range(ML_HEADS):
        cols = slice(h * hd, (h + 1) * hd)
        bc = bcol[:, h:h + 1]
        br = brow[h:h + 1, :]
        lir = gr[h:h + 1, :]
        lic = gc[:, h:h + 1]
        m_prev = m_scr[h:h + 1, 0:1]
        g = bc + m_prev
        dmat = jnp.where(causal, bc - br + lir, NEG)
        m_t = jnp.maximum(g, jnp.max(dmat, axis=-1, keepdims=True))
        w_intra = jnp.exp(dmat - m_t)
        w_inter = jnp.exp(g - m_t)
        qh = q_ref[:, cols]
        kh = k_ref[:, cols]
        vh = v_ref[:, cols]
        s = lax.dot_general(qh, kh, (((1,), (1,)), ((), ())), preferred_element_type=F32) * scale * w_intra
        c_old = c_scr[h]
        n_old = n_scr[h:h + 1, :]
        num = (jnp.dot(s.astype(BF16), vh, preferred_element_type=F32)
               + w_inter * jnp.dot(qh, c_old.astype(BF16), preferred_element_type=F32))
        den = (jnp.sum(s, axis=-1, keepdims=True)
               + w_inter * jnp.sum(qh.astype(F32) * n_old, axis=-1, keepdims=True))
        out = num / jnp.maximum(jnp.abs(den), jnp.exp(-m_t))

        b_last = bc[L - 1:L, :]
        dec = b_last - bc + lic
        m_new = jnp.maximum(b_last + m_prev, jnp.max(dec, axis=0, keepdims=True))
        ws = jnp.exp(dec - m_new)
        wc = jnp.exp(b_last + m_prev - m_new)
        kw = kh.astype(F32) * (ws * scale)
        c_scr[h] = wc * c_old + lax.dot_general(kw.astype(BF16), vh, (((0,), (0,)), ((), ())),
                                                preferred_element_type=F32)
        n_scr[h:h + 1, :] = wc * n_old + jnp.sum(kw, axis=0, keepdims=True)
        m_scr[h:h + 1, :] = jnp.broadcast_to(m_new, (1, m_scr.shape[1]))

        mu = jnp.mean(out, axis=-1, keepdims=True)
        dlt = out - mu
        var = jnp.mean(dlt * dlt, axis=-1, keepdims=True)
        hn = dlt * lax.rsqrt(var + LN_EPS) * ng_ref[:, cols]
        hn = hn + skip_ref[:, cols] * xc_ref[:, cols].astype(F32)
        o_ref[:, cols] = (_sigmoid(z_ref[:, cols].astype(F32)) * hn).astype(BF16)


def ml_chunk(q, k, v, xc, zz, gates, gates_t, norm_g, skip, bsz, seq):
    t, width = q.shape
    hd = width // ML_HEADS
    L = ML_L
    cpb = seq // L
    row = lambda b, c: (b * cpb + c, 0)
    act = lambda: pl.BlockSpec((L, width), row)
    return pl.pallas_call(
        _ml_chunk_kernel,
        grid=(bsz, cpb),
        in_specs=[
            act(), act(), act(), act(),
            pl.BlockSpec((L, width), lambda b, c: (b * cpb + c, 1)),
            pl.BlockSpec((L, 2 * ML_HEADS), row),
            pl.BlockSpec((2 * ML_HEADS, L), lambda b, c: (0, b * cpb + c)),
            _const_spec((1, width)), _const_spec((1, width)),
        ],
        out_specs=act(),
        out_shape=jax.ShapeDtypeStruct((t, width), BF16),
        scratch_shapes=[pltpu.VMEM((ML_HEADS, hd, hd), F32), pltpu.VMEM((ML_HEADS, hd), F32),
                        pltpu.VMEM((ML_HEADS, 128), F32)],
        compiler_params=_cparams(("parallel", "arbitrary")),
        name="ml_chunk",
    )(q, k, v, xc, zz, gates, gates_t, norm_g.reshape(1, width), skip.reshape(1, width))


def _router_kernel(h_ref, w_ref, b_ref, eid_ref, prob_ref, rank_ref, cnt_ref, cnt_scr):
    tr = h_ref.shape[0]
    ne = w_ref.shape[0]

    @pl.when(pl.program_id(0) == 0)
    def _():
        cnt_scr[...] = jnp.zeros_like(cnt_scr)

    logits = lax.dot_general(w_ref[...], _unpack_rows(h_ref[...]), (((1,), (1,)), ((), ())),
                             preferred_element_type=F32) + b_ref[...]
    sub = lax.broadcasted_iota(jnp.int32, (ne, tr), 0)
    work = logits
    sel = jnp.zeros((ne, tr), F32)
    hits = []
    top = None
    for k in range(TOP_K):
        m = jnp.max(work, axis=0, keepdims=True)
        idx = jnp.min(jnp.where(work == m, sub, ne), axis=0, keepdims=True)
        hit = sub == idx
        hits.append((hit, idx))
        sel = jnp.where(hit, 1.0, sel)
        work = jnp.where(hit, -jnp.inf, work)
        if k == 0:
            top = m
    e = sel * jnp.exp(logits - top)
    p = e / jnp.sum(e, axis=0, keepdims=True)

    ri = lax.broadcasted_iota(jnp.int32, (tr, tr), 0)
    ci = lax.broadcasted_iota(jnp.int32, (tr, tr), 1)
    strict_upper = (ri < ci).astype(BF16)
    before = jnp.dot(sel.astype(BF16), strict_upper, preferred_element_type=F32)
    rank = cnt_scr[...] + before
    cnt_scr[...] = cnt_scr[...] + jnp.sum(sel, axis=1, keepdims=True)
    cnt_ref[...] = cnt_scr[...].astype(jnp.int32)

    row = lax.broadcasted_iota(jnp.int32, (SUBLANES, tr), 0)
    eid = jnp.zeros((SUBLANES, tr), jnp.int32)
    prob = jnp.zeros((SUBLANES, tr), F32)
    rnk = jnp.zeros((SUBLANES, tr), F32)
    for k, (hit, idx) in enumerate(hits):
        slot = row == k
        eid = jnp.where(slot, idx, eid)
        prob = jnp.where(slot, jnp.sum(jnp.where(hit, p, 0.0), axis=0, keepdims=True), prob)
        rnk = jnp.where(slot, jnp.sum(jnp.where(hit, rank, 0.0), axis=0, keepdims=True), rnk)
    eid_ref[...] = eid
    prob_ref[...] = prob
    rank_ref[...] = rnk.astype(jnp.int32)


def router(h2, w_router, b_router, tr=512):
    t, dp = h2.shape
    d, ne = w_router.shape
    tr = min(tr, t)
    col = lambda i: (0, i)
    return pl.pallas_call(
        _router_kernel,
        grid=(t // tr,),
        in_specs=[pl.BlockSpec((tr, dp), lambda i: (i, 0)), _const_spec((ne, d)), _const_spec((ne, 1))],
        out_specs=[pl.BlockSpec((SUBLANES, tr), col), pl.BlockSpec((SUBLANES, tr), col),
                   pl.BlockSpec((SUBLANES, tr), col), pl.BlockSpec((ne, 1), lambda i: (0, 0))],
        out_shape=[jax.ShapeDtypeStruct((SUBLANES, t), jnp.int32), jax.ShapeDtypeStruct((SUBLANES, t), F32),
                   jax.ShapeDtypeStruct((SUBLANES, t), jnp.int32), jax.ShapeDtypeStruct((ne, 1), jnp.int32)],
        scratch_shapes=[pltpu.VMEM((ne, 1), F32)],
        compiler_params=_cparams(("arbitrary",)),
        name="router",
    )(h2, w_router.T.astype(BF16), b_router.reshape(ne, 1))


def _expert_kernel(te_ref, nu_ref, nx_ref, x_ref, wgu_hbm, bgu_ref, wd_hbm, bd_ref, o_ref,
                   wgu_f32, wd_f32, wgu_bf, wd_bf, sem, *, layer):
    i = pl.program_id(0)
    used = i < nu_ref[0]
    e = te_ref[i]
    new_expert = jnp.logical_or(i == 0, e != te_ref[jnp.maximum(i - 1, 0)])

    def fetch(expert):
        return (pltpu.make_async_copy(wgu_hbm.at[layer, expert], wgu_f32, sem.at[0]),
                pltpu.make_async_copy(wd_hbm.at[layer, expert], wd_f32, sem.at[1]))

    @pl.when(jnp.logical_and(used, i == 0))
    def _():
        for cp in fetch(e):
            cp.start()

    @pl.when(jnp.logical_and(used, new_expert))
    def _():
        for cp in fetch(e):
            cp.wait()
        wgu_bf[...] = wgu_f32[...].astype(BF16)
        wd_bf[...] = wd_f32[...].astype(BF16)
        nxt = nx_ref[i]

        @pl.when(nxt != e)
        def _():
            for cp in fetch(nxt):
                cp.start()

    @pl.when(used)
    def _():
        f = wd_bf.shape[0]
        gu = jnp.dot(_unpack_rows(x_ref[...]), wgu_bf[...], preferred_element_type=F32) + bgu_ref[...]
        gate = jnp.minimum(gu[:, :f], SWIGLU_LIMIT)
        up = jnp.clip(gu[:, f:], -SWIGLU_LIMIT, SWIGLU_LIMIT)
        act = gate * _sigmoid(SWIGLU_ALPHA * gate) * (up + 1.0)
        y = jnp.dot(act.astype(BF16), wd_bf[...], preferred_element_type=F32) + bd_ref[...]
        o_ref[...] = _pack_rows(y)

    @pl.when(jnp.logical_not(used))
    def _():
        o_ref[...] = jnp.zeros_like(o_ref)


def experts(xs, tile_e, n_used, tile_next, w_gu, b_gu, w_d, b_d, layer, deps=()):
    p, dp = xs.shape
    depth, ne, d, f2 = w_gu.shape
    f = f2 // 2
    te = EXPERT_TILE
    grid_spec = pltpu.PrefetchScalarGridSpec(
        num_scalar_prefetch=3,
        grid=(p // te,),
        in_specs=[
            pl.BlockSpec((te, dp), lambda i, e, n, nx: (jnp.minimum(i, jnp.maximum(n[0] - 1, 0)), 0)),
            pl.BlockSpec(memory_space=pl.ANY),
            pl.BlockSpec((None, None, 1, f2), lambda i, e, n, nx: (layer, e[i], 0, 0)),
            pl.BlockSpec(memory_space=pl.ANY),
            pl.BlockSpec((None, None, 1, d), lambda i, e, n, nx: (layer, e[i], 0, 0)),
        ] + _dep_specs(deps),
        out_specs=pl.BlockSpec((te, dp), lambda i, e, n, nx: (i, 0)),
        scratch_shapes=[pltpu.VMEM((d, f2), F32), pltpu.VMEM((f, d), F32),
                        pltpu.VMEM((d, f2), BF16), pltpu.VMEM((f, d), BF16),
                        pltpu.SemaphoreType.DMA((2,))],
    )
    return pl.pallas_call(
        _after(functools.partial(_expert_kernel, layer=layer), 8, deps),
        grid_spec=grid_spec,
        out_shape=jax.ShapeDtypeStruct((p, dp), jnp.uint32),
        compiler_params=_cparams(("arbitrary",)),
        name="experts",
    )(tile_e, n_used, tile_next, xs, w_gu, b_gu.reshape(depth, ne, 1, f2), w_d,
      b_d.reshape(depth, ne, 1, d), *deps)


def _combine_ln_kernel(y0_ref, y1_ref, y2_ref, y3_ref, p_ref, x_ref, g_ref, lng_ref, lnb_ref, xo_ref,
                       *, alpha):
    p = p_ref[...]
    y = None
    for k, y_ref in enumerate((y0_ref, y1_ref, y2_ref, y3_ref)):
        term = p[:, k:k + 1] * _unpack_rows(y_ref[...]).astype(F32)
        y = term if y is None else y + term
    v = alpha * x_ref[...] + (1.0 + g_ref[...]) * y
    xo_ref[...] = _layer_norm_rows(v, lng_ref[...], lnb_ref[...])


def combine_ln(ys, prob, x, mod4, ln_g, ln_b, seq, alpha, deps=(), tm=512):
    t, d = x.shape
    ne = prob.shape[1]
    rows_per_b = seq // tm
    row = lambda i: (i, 0)
    return pl.pallas_call(
        _after(functools.partial(_combine_ln_kernel, alpha=alpha), 9, deps),
        grid=(t // tm,),
        in_specs=[pl.BlockSpec((tm, d // 2), row)] * 4 + [
            pl.BlockSpec((tm, ne), row),
            pl.BlockSpec((tm, d), row),
            pl.BlockSpec((None, None, 1, d), lambda i: (i // rows_per_b, 5, 0, 0)),
            _const_spec((1, d)), _const_spec((1, d)),
        ] + _dep_specs(deps),
        out_specs=pl.BlockSpec((tm, d), row),
        out_shape=jax.ShapeDtypeStruct((t, d), F32),
        compiler_params=_cparams(("parallel",)),
        name="combine_ln",
    )(*ys, prob, x, mod4, ln_g.reshape(1, d), ln_b.reshape(1, d), *deps)


def moe_route(h2, w_router, b_router):
    t, d = h2.shape
    ne = w_router.shape[1]
    te = EXPERT_TILE
    eid, prob, rank, counts2d = router(h2, w_router, b_router)
    counts = counts2d[:, 0]
    padded = ((counts + te - 1) // te) * te
    ends = jnp.cumsum(padded)
    starts = ends - padded
    prob = prob.T
    eid4 = eid[:TOP_K]
    first_slot = jnp.zeros_like(eid4)
    for e in range(ne):
        first_slot = jnp.where(eid4 == e, starts[e], first_slot)
    dest4 = first_slot + rank[:TOP_K]
    n_slots = t * TOP_K + ne * te
    tok = jnp.broadcast_to(jnp.arange(t, dtype=jnp.int32)[None, :], (TOP_K, t))
    filler = jnp.arange(n_slots, dtype=jnp.int32) % t
    src = filler.at[dest4.reshape(-1)].set(tok.reshape(-1), unique_indices=True, mode=_IN_BOUNDS)
    n_tiles = n_slots // te
    tile_start = jnp.arange(n_tiles, dtype=jnp.int32) * te
    tile_e = jnp.sum((ends[None, :] <= tile_start[:, None]).astype(jnp.int32), axis=1)
    n_used = (ends[-1] // te).astype(jnp.int32).reshape(1)
    last_e = jnp.sum((ends <= ends[-1] - 1).astype(jnp.int32))
    tile_e = jnp.where(tile_start < ends[-1], tile_e, last_e).astype(jnp.int32)
    ids = jnp.arange(ne, dtype=jnp.int32)
    later = jnp.where((ids[None, :] > ids[:, None]) & (padded[None, :] > 0), ids[None, :], ne)
    nxt = jnp.min(later, axis=1)
    nxt = jnp.where(nxt == ne, ids, nxt)
    tile_next = jnp.sum(jnp.where(tile_e[:, None] == ids[None, :], nxt[None, :], 0), axis=1).astype(jnp.int32)
    return dict(src=src, dest4=dest4, prob=prob, tile_e=tile_e, n_used=n_used, tile_next=tile_next,
                order_token=counts2d)


def kernel(x, c, ada_w, ada_b, ln1_g, ln1_b, ln2_g, ln2_b, rg_w_in, rg_conv_w, rg_conv_b, rg_w_rgate, rg_b_rgate, rg_w_igate, rg_b_igate, rg_lam, rg_w_out, sg_w_in, sg_b_in, sg_ln_g, sg_ln_b, sg_w_sp, sg_b_sp, sg_w_out, sg_b_out, ml_w_in, ml_conv_w, ml_conv_b, ml_w_q, ml_w_k, ml_w_v, ml_w_gates, ml_b_gates, ml_skip, ml_norm_g, ml_w_out, moe_w_router, moe_b_router, moe_w_gate_up, moe_b_gate_up, moe_w_down, moe_b_down):
    bsz, seq, d = x.shape
    depth = ada_w.shape[0]
    alpha = (2 * depth) ** 0.25
    xfull = x.reshape(bsz * seq, d)
    mod = ada_mod(c, ada_w, ada_b)
    zero_bias = jnp.zeros((d,), F32)

    groups = BATCH_GROUPS if bsz % BATCH_GROUPS == 0 else 1
    gb = bsz // groups
    xs = [xfull] * groups
    offs = [g * gb * seq for g in range(groups)]

    for i in range(depth):
        kind, j = i % 3, i // 3
        if kind == 0:
            w_in = rg_w_in[j].astype(BF16)
            w_ri = _rg_gate_weights(rg_w_rgate[j], rg_w_igate[j])
            w_out, b_out = rg_w_out[j].astype(BF16), zero_bias
        elif kind == 1:
            w_in = sg_w_in[j].astype(BF16)
            w_out, b_out = sg_w_out[j].astype(BF16), sg_b_out[j]
        else:
            w_in = ml_w_in[j].astype(BF16)
            wq, wk, wv = (_ml_headwise_weights(w[j]) for w in (ml_w_q, ml_w_k, ml_w_v))
            w_out, b_out = ml_w_out[j].astype(BF16), zero_bias
        mods, routes, x_mid, h_mid = [], [], [], []
        prev = ()
        for g in range(groups):
            mod4 = mod[i, g * gb:(g + 1) * gb].reshape(gb, 6, 1, d)
            xt, off = xs[g], offs[g]
            if kind == 0:
                z = mm_in(xt, mod4, 1, 0, w_in, seq, off, deps=prev)
                a = rg_core(z, rg_conv_w[j], rg_conv_b[j], w_ri, rg_b_rgate[j], rg_b_igate[j], rg_lam[j],
                            gb, seq)
            elif kind == 1:
                u, v = sg_in(xt, mod4, w_in, sg_b_in[j], sg_ln_g[j], sg_ln_b[j], seq, deps=prev)
                a = sg_mix(u, v, sg_w_sp[j], sg_b_sp[j])
            else:
                zz = mm_in(xt, mod4, 1, 0, w_in, seq, off, deps=prev)
                q, k, v, xc, gates = ml_qkv(zz, ml_conv_w[j], ml_conv_b[j], wq, wk, wv,
                                            ml_w_gates[j], ml_b_gates[j], gb, seq)
                a = ml_chunk(q, k, v, xc, zz, gates, gates.T, ml_norm_g[j], ml_skip[j], gb, seq)
            xt, h2 = mm_res_ln(a, w_out, b_out, xt, mod4, 2, 4, 3, ln1_g[i], ln1_b[i], seq, alpha, off)
            route = moe_route(h2, moe_w_router[i], moe_b_router[i])
            prev = (route["order_token"],)
            mods.append(mod4); routes.append(route); x_mid.append(xt); h_mid.append(h2)
        ysorted = []
        for g in range(groups):
            r = routes[g]
            rows = h_mid[g].at[r["src"]].get(mode=_IN_BOUNDS)
            ysorted.append(experts(rows, r["tile_e"], r["n_used"], r["tile_next"], moe_w_gate_up,
                                   moe_b_gate_up, moe_w_down, moe_b_down, i, deps=prev))
            prev = (ysorted[g],)
        for g in range(groups):
            r = routes[g]
            ys = [ysorted[g].at[r["dest4"][k]].get(mode=_IN_BOUNDS, unique_indices=True)
                  for k in range(TOP_K)]
            xs[g] = combine_ln(ys, r["prob"], x_mid[g], mods[g], ln2_g[i], ln2_b[i], seq, alpha, deps=prev)
            prev = (xs[g],)
            offs[g] = 0
    return jnp.concatenate(xs, axis=0).reshape(bsz, seq, d)
```
